```python
import jax, jax.numpy as jnp
from jax import lax
import numpy as np

D_MODEL = 2048
BATCH = 16
SEQ = 256
DEPTH = 2
DEC_BATCH = 8
DEC_SEQ = 2048
PAST_LEN = 512

GRID_W = 64
N_MOD = 6
EPS = 1e-6
DN_HEADS = 8
DN_DK = 128
DN_DV = 128
DN_CHUNK = 64
CONV_K = 3
QK_W = DN_HEADS * DN_DK
V_W = DN_HEADS * DN_DV
GM_GROUPS = 8
GM_CH = 128
GM_WIDTH = GM_GROUPS * GM_CH
GM_CHUNK = 128
N_IN = 2 * QK_W + 2 * V_W + 4 * DN_HEADS + 2 * GM_WIDTH + 2 * D_MODEL
PEER_HEADS = 8
PEER_NKEYS = 128
PEER_EXPERTS = PEER_NKEYS * PEER_NKEYS
PEER_DKEY = 256
PEER_TOPK = 16
PEER_TOKEN_BLOCK = 128

kernel_name = "hybrid_deltanet_gmlp_peer_diffusion_step"


def rmsnorm(x, w):
    xf = x.astype(jnp.float32)
    y = xf * lax.rsqrt(jnp.mean(xf * xf, axis=-1, keepdims=True) + EPS)
    return (y * w.astype(jnp.float32)).astype(x.dtype)


def l2norm(x):
    xf = x.astype(jnp.float32)
    return (xf * lax.rsqrt(jnp.sum(xf * xf, axis=-1, keepdims=True) + EPS)).astype(x.dtype)


def split_last(a, sizes):
    return jnp.split(a, np.cumsum(sizes)[:-1].tolist(), axis=-1)


def grid_pos_embed(T, dtype):
    rows = T // GRID_W
    r, col = jnp.meshgrid(jnp.arange(rows, dtype=jnp.float32), jnp.arange(GRID_W, dtype=jnp.float32), indexing='ij')
    quarter = D_MODEL // 4
    freq = 1.0 / (10000.0 ** (jnp.arange(quarter, dtype=jnp.float32) / quarter))

    def enc(p):
        ang = p.reshape(-1, 1) * freq
        return jnp.concatenate([jnp.sin(ang), jnp.cos(ang)], axis=-1)

    return jnp.concatenate([enc(r), enc(col)], axis=-1).astype(dtype)


def adaln(cond, w_mod, b_mod):
    m = jax.nn.silu(cond) @ w_mod + b_mod
    return m.reshape(cond.shape[0], N_MOD, D_MODEL)


def short_conv(x, w):
    C = x.shape[-1]
    return lax.conv_general_dilated(x, w[:, None, :].astype(x.dtype), window_strides=(1,),
                                    padding=[(CONV_K // 2, CONV_K // 2)],
                                    dimension_numbers=('NWC', 'WIO', 'NWC'), feature_group_count=C)


def gated_delta_chunked(q, k, v, g, beta, s0):
    B, T, H, _ = q.shape
    n = T // DN_CHUNK
    f32 = jnp.float32

    def to_chunks(a):
        a = a.astype(f32).reshape((B, n, DN_CHUNK, H) + a.shape[3:])
        return jnp.moveaxis(a, (1, 3), (0, 2))

    qc = to_chunks(q) * (DN_DK ** -0.5)
    kc = to_chunks(k)
    vc = to_chunks(v)
    bc = to_chunks(beta)
    gc = jnp.cumsum(to_chunks(g), axis=-1)
    idx = jnp.arange(DN_CHUNK)
    causal = idx[:, None] >= idx[None, :]
    strict = idx[:, None] > idx[None, :]
    decay = jnp.exp(jnp.where(causal, gc[..., :, None] - gc[..., None, :], -jnp.inf))
    kb = kc * bc[..., None]
    m = jnp.where(strict, jnp.einsum('nbhid,nbhjd->nbhij', kb, kc) * decay, 0.0)
    tri = m + jnp.eye(DN_CHUNK, dtype=f32)
    rhs = jnp.concatenate([vc * bc[..., None], kb * jnp.exp(gc)[..., None]], axis=-1)
    sol = lax.linalg.triangular_solve(tri, rhs, left_side=True, lower=True, unit_diagonal=True)
    u_c, w_c = sol[..., :DN_DV], sol[..., DN_DV:]
    qk = jnp.einsum('nbhid,nbhjd->nbhij', qc, kc) * decay

    def step(S, xs):
        qi, ki, ui, wi, gi, qki = xs
        v_new = ui - jnp.einsum('bhcd,bhde->bhce', wi, S)
        o = (jnp.einsum('bhcd,bhde->bhce', qi * jnp.exp(gi)[..., None], S)
             + jnp.einsum('bhij,bhje->bhie', qki, v_new))
        glast = gi[..., -1]
        S = (S * jnp.exp(glast)[..., None, None]
             + jnp.einsum('bhcd,bhce->bhde', ki * jnp.exp(glast[..., None] - gi)[..., None], v_new))
        return S, o

    S, o = lax.scan(step, s0.astype(f32), (qc, kc, u_c, w_c, gc, qk))
    o = jnp.moveaxis(o, (0, 2), (1, 3)).reshape(B, T, H, DN_DV)
    return o.astype(v.dtype), S.astype(v.dtype)


def chunk_spatial_gate(u, vg, norm_w, w_s, b_s):
    B, T, _ = u.shape
    n = T // GM_CHUNK
    u = jax.nn.gelu(u)
    vg = rmsnorm(jax.nn.gelu(vg), norm_w).reshape(B, n, GM_CHUNK, GM_GROUPS, GM_CH)
    mixed = jnp.einsum('gts,bnsgc->bntgc', w_s, vg) + b_s.T[None, None, :, :, None]
    return u * mixed.reshape(B, T, GM_WIDTH)


def peer(h, w_query, sub_keys, expert_u, expert_v):
    B, T, D = h.shape
    q = jnp.einsum('btd,dhk->bthk', h, w_query).reshape(B, T, PEER_HEADS, 2, PEER_DKEY // 2)
    s = jnp.einsum('bthpk,hpnk->bthpn', q, sub_keys)
    sv, si = lax.top_k(s, PEER_TOPK)
    cand = (sv[..., 0, :, None] + sv[..., 1, None, :]).reshape(B, T, PEER_HEADS, PEER_TOPK * PEER_TOPK)
    cidx = (si[..., 0, :, None] * PEER_NKEYS + si[..., 1, None, :]).reshape(B, T, PEER_HEADS, PEER_TOPK * PEER_TOPK)
    best, pos = lax.top_k(cand, PEER_TOPK)
    eidx = jnp.take_along_axis(cidx, pos, axis=-1)
    gates = jax.nn.softmax(best.astype(jnp.float32), axis=-1).astype(h.dtype)
    nb = (B * T) // PEER_TOKEN_BLOCK
    hb = h.reshape(nb, PEER_TOKEN_BLOCK, D)
    ib = eidx.reshape(nb, PEER_TOKEN_BLOCK, PEER_HEADS * PEER_TOPK)
    gb = gates.reshape(nb, PEER_TOKEN_BLOCK, PEER_HEADS * PEER_TOPK)

    def block(args):
        hx, ix, gx = args
        act = jax.nn.gelu(jnp.einsum('nd,ned->ne', hx, expert_u[ix]))
        return jnp.einsum('ne,ned->nd', act * gx, expert_v[ix])

    return lax.map(block, (hb, ib, gb)).reshape(B, T, D)


def trunk_layer(x, mod, s0, lp):
    B, T, _ = x.shape
    sh1, sc1, gt1, sh2, sc2, gt2 = [mod[:, i, None, :] for i in range(N_MOD)]
    h = rmsnorm(x, lp['norm_mix']) * (1 + sc1) + sh1
    proj = h @ lp['w_in']
    qkv, z, bf, bb, af, ab, u, vg, ga, gb = split_last(
        proj, [2 * QK_W + V_W, V_W, DN_HEADS, DN_HEADS, DN_HEADS, DN_HEADS, GM_WIDTH, GM_WIDTH, D_MODEL, D_MODEL])
    qkv = jax.nn.silu(short_conv(qkv, lp['conv_w']))
    q, k, v = split_last(qkv, [QK_W, QK_W, V_W])
    q = l2norm(q.reshape(B, T, DN_HEADS, DN_DK))
    k = l2norm(k.reshape(B, T, DN_HEADS, DN_DK))
    v = v.reshape(B, T, DN_HEADS, DN_DV)
    beta = jax.nn.sigmoid(jnp.stack([bf, bb], 0).astype(jnp.float32))
    a = jnp.stack([af, ab], 0).astype(jnp.float32)
    g = (-jnp.exp(lp['dn_a_log'].astype(jnp.float32))[:, None, None, :]
         * jax.nn.softplus(a + lp['dn_dt_bias'].astype(jnp.float32)[:, None, None, :]))
    o_f, s_f = gated_delta_chunked(q, k, v, g[0], beta[0], s0[:, 0])
    flip = lambda t: jnp.flip(t, axis=1)
    o_b, s_b = gated_delta_chunked(flip(q), flip(k), flip(v), flip(g[1]), flip(beta[1]), s0[:, 1])
    o = o_f + flip(o_b)
    o = rmsnorm(o, lp['dn_norm']) * jax.nn.silu(z.reshape(B, T, DN_HEADS, DN_DV))
    p_a = o.reshape(B, T, V_W) @ lp['w_branch_a']
    p_b = chunk_spatial_gate(u, vg, lp['gm_norm'], lp['gm_w_s'], lp['gm_b_s']) @ lp['w_branch_b']
    merged = jax.nn.sigmoid(ga) * p_a + jax.nn.sigmoid(gb) * p_b
    x = x + gt1 * (merged @ lp['w_out'])
    h2 = rmsnorm(x, lp['norm_ffn']) * (1 + sc2) + sh2
    x = x + gt2 * peer(h2, lp['peer_w_query'], lp['peer_sub_keys'], lp['peer_u'], lp['peer_v'])
    return x, jnp.stack([s_f, s_b], axis=1)


def setup_inputs(seed: int = 0) -> dict:
    key = jax.random.key(seed)
    ks = jax.random.split(key, 32)
    f32 = jnp.float32
    L, D = DEPTH, D_MODEL
    nrm = lambda k, shape, scale: jax.random.normal(k, shape, f32) * scale
    dt = jnp.exp(jax.random.uniform(ks[11], (L, 2, DN_HEADS), f32, np.log(1e-3), np.log(1e-1)))
    return {
        'x_prompt': nrm(ks[0], (BATCH, SEQ, D), 1.0),
        'x_sample': nrm(ks[1], (DEC_BATCH, DEC_SEQ, D), 1.0),
        'state_delta': nrm(ks[2], (DEC_BATCH, DEPTH, 2, DN_HEADS, DN_DK, DN_DV), DN_DK ** -0.5),
        'c': nrm(ks[3], (DEC_BATCH, D), 1.0),
        'c_ctx': nrm(ks[4], (D,), 1.0),
        'w_mod': nrm(ks[5], (L, D, N_MOD * D), D ** -0.5),
        'b_mod': nrm(ks[6], (L, N_MOD * D), 0.02),
        'norm_mix': 1.0 + nrm(ks[7], (L, D), 0.02),
        'w_in': nrm(ks[8], (L, D, N_IN), D ** -0.5),
        'conv_w': nrm(ks[9], (L, CONV_K, 2 * QK_W + V_W), CONV_K ** -0.5),
        'dn_a_log': jnp.log(jax.random.uniform(ks[10], (L, 2, DN_HEADS), f32, 1.0, 16.0)),
        'dn_dt_bias': dt + jnp.log(-jnp.expm1(-dt)),
        'dn_norm': 1.0 + nrm(ks[12], (L, DN_DV), 0.02),
        'gm_norm': 1.0 + nrm(ks[13], (L, GM_WIDTH), 0.02),
        'gm_w_s': nrm(ks[14], (L, GM_GROUPS, GM_CHUNK, GM_CHUNK), GM_CHUNK ** -0.5),
        'gm_b_s': 1.0 + nrm(ks[15], (L, GM_GROUPS, GM_CHUNK), 0.02),
        'w_branch_a': nrm(ks[16], (L, V_W, D), V_W ** -0.5),
        'w_branch_b': nrm(ks[17], (L, GM_WIDTH, D), GM_WIDTH ** -0.5),
        'w_out': nrm(ks[18], (L, D, D), D ** -0.5),
        'norm_ffn': 1.0 + nrm(ks[19], (L, D), 0.02),
        'peer_w_query': nrm(ks[20], (L, D, PEER_HEADS, PEER_DKEY), D ** -0.5),
        'peer_sub_keys': nrm(ks[21], (L, PEER_HEADS, 2, PEER_NKEYS, PEER_DKEY // 2), (PEER_DKEY // 2) ** -0.5),
        'peer_u': nrm(ks[22], (L, PEER_EXPERTS, D), D ** -0.5),
        'peer_v': nrm(ks[23], (L, PEER_EXPERTS, D), PEER_HEADS ** -0.5),
        'norm_final': 1.0 + nrm(ks[24], (D,), 0.02),
    }


def reference(x_prompt, x_sample, state_delta, c, c_ctx, w_mod, b_mod, norm_mix, w_in, conv_w,
              dn_a_log, dn_dt_bias, dn_norm, gm_norm, gm_w_s, gm_b_s, w_branch_a, w_branch_b, w_out,
              norm_ffn, peer_w_query, peer_sub_keys, peer_u, peer_v, norm_final):
    xp = x_prompt
    xs = x_sample + grid_pos_embed(x_sample.shape[1], x_sample.dtype)[None]
    ctx_s0 = jnp.zeros((x_prompt.shape[0], 2, DN_HEADS, DN_DK, DN_DV), x_prompt.dtype)
    new_states = []
    for l in range(DEPTH):
        lp = {
            'norm_mix': norm_mix[l], 'w_in': w_in[l], 'conv_w': conv_w[l], 'dn_a_log': dn_a_log[l],
            'dn_dt_bias': dn_dt_bias[l], 'dn_norm': dn_norm[l], 'gm_norm': gm_norm[l], 'gm_w_s': gm_w_s[l],
            'gm_b_s': gm_b_s[l], 'w_branch_a': w_branch_a[l], 'w_branch_b': w_branch_b[l], 'w_out': w_out[l],
            'norm_ffn': norm_ffn[l], 'peer_w_query': peer_w_query[l], 'peer_sub_keys': peer_sub_keys[l],
            'peer_u': peer_u[l], 'peer_v': peer_v[l],
        }
        mod_ctx = adaln(c_ctx[None, :], w_mod[l], b_mod[l])
        mod_lat = adaln(c, w_mod[l], b_mod[l])
        xp, st = trunk_layer(xp, mod_ctx, ctx_s0, lp)
        new_states.append(st)
        xs, _ = trunk_layer(xs, mod_lat, state_delta[:, l], lp)
    y_prompt = rmsnorm(xp, norm_final)
    y_sample = rmsnorm(xs, norm_final)
    new_state_delta = jnp.stack(new_states, axis=1)
    return (y_prompt, y_sample, new_state_delta)
```

```python
import functools
import math

import numpy as np
import jax
import jax.numpy as jnp
from jax import lax
from jax.experimental import pallas as pl
from jax.experimental.pallas import tpu as pltpu

F32 = jnp.float32
BF16 = jnp.bfloat16

EPS = 1e-6
N_MOD = 6
GRID_W = 64
DN_HEADS = 8
DN_DK = 128
DN_CHUNK = 64
GM_GROUPS = 8
GM_CH = 128
GM_CHUNK = 128
PEER_HEADS = 8
PEER_NKEYS = 128
PEER_TOPK = 16
LANE = 128
SUBLANE = 8
VMEM_LIMIT = 56 * 1024 * 1024
NEG_BIG = -1e30

_HI = lax.Precision.HIGHEST


def _cparams(sem):
    return pltpu.CompilerParams(dimension_semantics=sem, vmem_limit_bytes=VMEM_LIMIT)


def _pick(n, cap):
    b = 1
    while b * 2 <= cap and n % (b * 2) == 0:
        b *= 2
    return b


def _dot(a, b):
    return jnp.dot(a.astype(BF16), b.astype(BF16), preferred_element_type=F32)


def _dot_nt(a, b):
    return lax.dot_general(a.astype(BF16), b.astype(BF16), (((1,), (1,)), ((), ())),
                           preferred_element_type=F32)


def _dot_tn(a, b):
    return lax.dot_general(a.astype(BF16), b.astype(BF16), (((0,), (0,)), ((), ())),
                           preferred_element_type=F32)


def _gelu(x):
    return 0.5 * x * (1.0 + jnp.tanh(math.sqrt(2.0 / math.pi) * (x + 0.044715 * (x * x * x))))


def _sigmoid(x):
    return 1.0 / (1.0 + jnp.exp(-x))


def _softplus(x):
    return jnp.maximum(x, 0.0) + jnp.log(1.0 + jnp.exp(-jnp.abs(x)))


def _mod_kernel(c_ref, w_ref, b_ref, o_ref):
    c = c_ref[...]
    o_ref[...] = _dot(c * _sigmoid(c), w_ref[...]) + b_ref[...]


def _modulation(cond, w_mod, b_mod):
    L, D, N = w_mod.shape
    R = cond.shape[0]
    tn = _pick(N, 1024)
    return pl.pallas_call(
        _mod_kernel,
        grid=(L, N // tn),
        in_specs=[pl.BlockSpec((R, D), lambda l, j: (0, 0)),
                  pl.BlockSpec((None, D, tn), lambda l, j: (l, 0, j)),
                  pl.BlockSpec((None, 1, tn), lambda l, j: (l, 0, j))],
        out_specs=pl.BlockSpec((None, R, tn), lambda l, j: (l, 0, j)),
        out_shape=jax.ShapeDtypeStruct((L, R, N), F32),
        compiler_params=_cparams(("parallel", "parallel")),
        name="adaln_mod",
    )(cond, w_mod, b_mod.reshape(L, 1, N))


class _Geom:
    def __init__(self, b_ctx, t_ctx, b_lat, t_lat):
        self.b_ctx, self.t_ctx, self.b_lat, self.t_lat = b_ctx, t_ctx, b_lat, t_lat
        self.n_ctx = b_ctx * t_ctx
        self.n_lat = b_lat * t_lat
        self.n_tok = self.n_ctx + self.n_lat

    def block(self, cap):
        return _pick(math.gcd(self.t_ctx, self.t_lat), cap)

    def mod_row(self, tb):
        n_ctx_blocks = self.n_ctx // tb
        per_seq = self.t_lat // tb

        def row(i):
            return jnp.where(i < n_ctx_blocks, 0, 1 + (i - n_ctx_blocks) // per_seq)

        return row


def _mod_spec(geom, tb, which, ngrid):
    row = geom.mod_row(tb)
    if ngrid == 1:
        return pl.BlockSpec((None, None, 1, None), lambda i: (row(i), which, 0, 0))
    return pl.BlockSpec((None, None, 1, None), lambda i, j: (row(i), which, 0, 0))


def _inproj_kernel(x_ref, nw_ref, sc_ref, sh_ref, w_ref, o_ref, h_scr):
    @pl.when(pl.program_id(1) == 0)
    def _():
        x = x_ref[...]
        ms = jnp.mean(x * x, axis=-1, keepdims=True)
        y = x * lax.rsqrt(ms + EPS) * nw_ref[...]
        h_scr[...] = (y * (1.0 + sc_ref[...]) + sh_ref[...]).astype(BF16)

    o_ref[...] = jnp.dot(h_scr[...], w_ref[...], preferred_element_type=F32)


def _mod_block(geom, tb, which, D, ngrid):
    row = geom.mod_row(tb)
    if ngrid == 1:
        return pl.BlockSpec((None, None, 1, D), lambda i: (row(i), which, 0, 0))
    return pl.BlockSpec((None, None, 1, D), lambda i, j: (row(i), which, 0, 0))


def _in_projection(x, norm_w, mod, w_bf16, geom):
    n_tok, D = x.shape
    N = w_bf16.shape[1]
    tb = geom.block(1024)
    tn = 1152 if N % 1152 == 0 else _pick(N, 1024)
    return pl.pallas_call(
        _inproj_kernel,
        grid=(n_tok // tb, N // tn),
        in_specs=[pl.BlockSpec((tb, D), lambda i, j: (i, 0)),
                  pl.BlockSpec((1, D), lambda i, j: (0, 0)),
                  _mod_block(geom, tb, 1, D, 2),
                  _mod_block(geom, tb, 0, D, 2),
                  pl.BlockSpec((D, tn), lambda i, j: (0, j))],
        out_specs=pl.BlockSpec((tb, tn), lambda i, j: (i, j)),
        out_shape=jax.ShapeDtypeStruct((n_tok, N), F32),
        scratch_shapes=[pltpu.VMEM((tb, D), BF16)],
        compiler_params=_cparams(("parallel", "arbitrary")),
        name="norm_in_proj",
    )(x, norm_w.reshape(1, D), mod, mod, w_bf16)


CONV_BLK = 256


def _conv_kernel(cur_ref, prev_ref, next_ref, w_ref, o_ref, *, geom):
    i = pl.program_id(0)
    j = pl.program_id(1)
    n_ctx_blocks = geom.n_ctx // CONV_BLK
    bps_ctx = geom.t_ctx // CONV_BLK
    bps_lat = geom.t_lat // CONV_BLK
    pos = jnp.where(i < n_ctx_blocks, i % bps_ctx, (i - n_ctx_blocks) % bps_lat)
    per = jnp.where(i < n_ctx_blocks, bps_ctx, bps_lat)
    is_start = pos == 0
    is_end = pos == per - 1

    x = cur_ref[...]
    prev_row = jnp.where(is_start, 0.0, prev_ref[SUBLANE - 1:SUBLANE, :])
    next_row = jnp.where(is_end, 0.0, next_ref[0:1, :])
    rid = lax.broadcasted_iota(jnp.int32, x.shape, 0)
    xm1 = jnp.where(rid == 0, prev_row, pltpu.roll(x, 1, axis=0))
    xp1 = jnp.where(rid == CONV_BLK - 1, next_row, pltpu.roll(x, CONV_BLK - 1, axis=0))
    w = w_ref[...]
    y = w[0:1, :] * xm1 + w[1:2, :] * x + w[2:3, :] * xp1
    y = y * _sigmoid(y)

    @pl.when(j == 2)
    def _():
        o_ref[...] = y

    @pl.when(j < 2)
    def _():
        scale = jnp.where(j == 0, DN_DK ** -0.5, 1.0).astype(F32)
        for h in range(DN_HEADS):
            seg = y[:, h * DN_DK:(h + 1) * DN_DK]
            ss = jnp.sum(seg * seg, axis=-1, keepdims=True)
            o_ref[:, h * DN_DK:(h + 1) * DN_DK] = seg * (lax.rsqrt(ss + EPS) * scale)


def _conv_qkv(proj, conv_w, geom):
    n_tok = proj.shape[0]
    width = DN_HEADS * DN_DK
    nb8 = n_tok // SUBLANE
    r = CONV_BLK // SUBLANE
    return pl.pallas_call(
        functools.partial(_conv_kernel, geom=geom),
        grid=(n_tok // CONV_BLK, 3),
        in_specs=[pl.BlockSpec((CONV_BLK, width), lambda i, j: (i, j)),
                  pl.BlockSpec((SUBLANE, width), lambda i, j: (jnp.maximum(i * r - 1, 0), j)),
                  pl.BlockSpec((SUBLANE, width), lambda i, j: (jnp.minimum((i + 1) * r, nb8 - 1), j)),
                  pl.BlockSpec((3, width), lambda i, j: (0, j))],
        out_specs=pl.BlockSpec((CONV_BLK, width), lambda i, j: (i, j)),
        out_shape=jax.ShapeDtypeStruct((n_tok, 3 * width), F32),
        compiler_params=_cparams(("parallel", "parallel")),
        name="conv_silu_l2",
    )(proj, proj, proj, conv_w)


DN_WIN = 256
_LEVELS = (1, 2, 4, 8, 16, 32)


def _dn_constants(reverse):
    n = DN_CHUNK
    r = np.arange(n)[:, None]
    c = np.arange(n)[None, :]
    before = (c > r) if reverse else (c < r)
    incl = before | (r == c)
    lvl = []
    for s in _LEVELS:
        same = (r // (2 * s)) == (c // (2 * s))
        if reverse:
            m = same & ((r % (2 * s)) < s) & ((c % (2 * s)) >= s)
        else:
            m = same & ((r % (2 * s)) >= s) & ((c % (2 * s)) < s)
        lvl.append(m)
    masks = np.stack([before, incl] + lvl).astype(np.float32)
    cum_col = incl.astype(np.float32)
    return jnp.asarray(masks), jnp.asarray(cum_col), jnp.asarray(cum_col.T)


def _dn_kernel(q_ref, k_ref, v_ref, sm_ref, s0_ref, pa_ref, pdt_ref, msk_ref, cc_ref, cr_ref,
               o_ref, sfin_ref, s_scr, *, reverse, d, n_win):
    wi = pl.program_id(1)

    @pl.when(wi == 0)
    def _():
        s_scr[...] = s0_ref[...]

    sm = sm_ref[...]
    g_all = pa_ref[...] * _softplus(sm + pdt_ref[...])
    beta_all = _sigmoid(sm)
    g_t = g_all.T
    before = msk_ref[0]
    incl = msk_ref[1]
    eye = incl - before
    cum_col = cc_ref[...]
    cum_row = cr_ref[...]
    n_chunks = DN_WIN // DN_CHUNK
    order = range(n_chunks - 1, -1, -1) if reverse else range(n_chunks)
    last = 0 if reverse else DN_CHUNK - 1
    for c in order:
        r0 = c * DN_CHUNK
        gc_col = jnp.dot(cum_col, g_all[r0:r0 + DN_CHUNK, :], precision=_HI, preferred_element_type=F32)
        gc_row = jnp.dot(g_t[:, r0:r0 + DN_CHUNK], cum_row, precision=_HI, preferred_element_type=F32)
        for h in range(DN_HEADS):
            la = 16 + 8 * d + h
            lb = 8 * d + h
            gcc = gc_col[:, la:la + 1]
            gcr = gc_row[la:la + 1, :]
            bcol = beta_all[r0:r0 + DN_CHUNK, lb:lb + 1]
            dec = jnp.exp(jnp.where(incl > 0, gcc - gcr, NEG_BIG))
            cs = slice(h * DN_DK, (h + 1) * DN_DK)
            qh = q_ref[r0:r0 + DN_CHUNK, cs]
            kh = k_ref[r0:r0 + DN_CHUNK, cs]
            vh = v_ref[r0:r0 + DN_CHUNK, cs]
            kk = _dot_nt(kh, kh)
            qk = _dot_nt(qh, kh)
            m = (bcol * kk) * (dec * before)
            t = eye - m * msk_ref[2]
            for li in range(1, len(_LEVELS)):
                cl = m * msk_ref[2 + li]
                t = t - _dot(_dot(t, cl), t)
            egc = jnp.exp(gcc)
            rhs = jnp.concatenate([vh * bcol, kh * (bcol * egc)], axis=1)
            sol = _dot(t, rhs)
            u = sol[:, :DN_DK]
            w = sol[:, DN_DK:]
            s = s_scr[h]
            v_new = u - _dot(w, s)
            o_ref[r0:r0 + DN_CHUNK, cs] = _dot(qh * egc, s) + _dot(qk * dec, v_new)
            gl = gcc[last:last + 1, :]
            s_scr[h] = s * jnp.exp(gl) + _dot_tn(kh * jnp.exp(gl - gcc), v_new)

    @pl.when(wi == n_win - 1)
    def _():
        sfin_ref[...] = s_scr[...]


def _deltanet_dir(qkv, proj, s0, a_log, dt_bias, *, reverse, d, tok_off, n_seq, t_seq, small_blk):
    n_tok = qkv.shape[0]
    width = DN_HEADS * DN_DK
    n_win = t_seq // DN_WIN
    blk0 = tok_off // DN_WIN
    masks, cum_col, cum_row = _dn_constants(reverse)
    lanes = jnp.arange(LANE)
    hd = jnp.clip(lanes - (16 + 8 * d), 0, DN_HEADS - 1)
    valid = (lanes >= 16 + 8 * d) & (lanes < 24 + 8 * d)
    pa = jnp.where(valid, -jnp.exp(a_log.astype(F32))[hd], 0.0).reshape(1, LANE)
    pdt = jnp.where(valid, dt_bias.astype(F32)[hd], 0.0).reshape(1, LANE)

    def tok_blk(b, w):
        w_eff = (n_win - 1 - w) if reverse else w
        return blk0 + b * n_win + w_eff

    dk2 = DN_DK
    return pl.pallas_call(
        functools.partial(_dn_kernel, reverse=reverse, d=d, n_win=n_win),
        grid=(n_seq, n_win),
        in_specs=[pl.BlockSpec((DN_WIN, width), lambda b, w: (tok_blk(b, w), 0)),
                  pl.BlockSpec((DN_WIN, width), lambda b, w: (tok_blk(b, w), 1)),
                  pl.BlockSpec((DN_WIN, width), lambda b, w: (tok_blk(b, w), 2)),
                  pl.BlockSpec((DN_WIN, LANE), lambda b, w: (tok_blk(b, w), small_blk)),
                  pl.BlockSpec((None, DN_HEADS, dk2, dk2), lambda b, w: (b, 0, 0, 0)),
                  pl.BlockSpec((1, LANE), lambda b, w: (0, 0)),
                  pl.BlockSpec((1, LANE), lambda b, w: (0, 0)),
                  pl.BlockSpec(masks.shape, lambda b, w: (0, 0, 0)),
                  pl.BlockSpec(cum_col.shape, lambda b, w: (0, 0)),
                  pl.BlockSpec(cum_row.shape, lambda b, w: (0, 0))],
        out_specs=[pl.BlockSpec((DN_WIN, width), lambda b, w: (tok_blk(b, w) - blk0, 0)),
                   pl.BlockSpec((None, DN_HEADS, dk2, dk2), lambda b, w: (b, 0, 0, 0))],
        out_shape=[jax.ShapeDtypeStruct((n_seq * t_seq, width), F32),
                   jax.ShapeDtypeStruct((n_seq, DN_HEADS, dk2, dk2), F32)],
        scratch_shapes=[pltpu.VMEM((DN_HEADS, dk2, dk2), F32)],
        compiler_params=_cparams(("parallel", "arbitrary")),
        name=f"deltanet_{'bwd' if reverse else 'fwd'}",
    )(qkv, qkv, qkv, proj, s0, pa, pdt, masks, cum_col, cum_row)


def _gmlp_kernel(u_ref, vg_ref, nw_ref, ws_ref, bias_ref, o_ref, *, tb):
    vg = _gelu(vg_ref[...])
    ms = jnp.mean(vg * vg, axis=-1, keepdims=True)
    vn = (vg * lax.rsqrt(ms + EPS) * nw_ref[...]).astype(BF16)
    for ch in range(tb // GM_CHUNK):
        rs = slice(ch * GM_CHUNK, (ch + 1) * GM_CHUNK)
        for g in range(GM_GROUPS):
            cs = slice(g * GM_CH, (g + 1) * GM_CH)
            mixed = jnp.dot(ws_ref[g], vn[rs, cs], preferred_element_type=F32) + bias_ref[g]
            o_ref[rs, cs] = (_gelu(u_ref[rs, cs]) * mixed).astype(BF16)


def _gmlp(proj, gm_norm, w_s, b_s, u_blk, vg_blk):
    n_tok = proj.shape[0]
    width = GM_GROUPS * GM_CH
    tb = 256
    bias = jnp.broadcast_to(b_s.astype(F32)[:, :, None], (GM_GROUPS, GM_CHUNK, GM_CH))
    return pl.pallas_call(
        functools.partial(_gmlp_kernel, tb=tb),
        grid=(n_tok // tb,),
        in_specs=[pl.BlockSpec((tb, width), lambda i: (i, u_blk)),
                  pl.BlockSpec((tb, width), lambda i: (i, vg_blk)),
                  pl.BlockSpec((1, width), lambda i: (0, 0)),
                  pl.BlockSpec((GM_GROUPS, GM_CHUNK, GM_CHUNK), lambda i: (0, 0, 0)),
                  pl.BlockSpec((GM_GROUPS, GM_CHUNK, GM_CH), lambda i: (0, 0, 0))],
        out_specs=pl.BlockSpec((tb, width), lambda i: (i, 0)),
        out_shape=jax.ShapeDtypeStruct((n_tok, width), BF16),
        compiler_params=_cparams(("parallel",)),
        name="gmlp_gate",
    )(proj, proj, gm_norm.reshape(1, width), w_s.astype(BF16), bias)


def _merge_kernel(of_ref, ob_ref, z_ref, sb_ref, ga_ref, gb_ref, dnw_ref, wa_ref, wb_ref, o_ref, on_scr):
    for h in range(DN_HEADS):
        cs = slice(h * DN_DK, (h + 1) * DN_DK)
        o = of_ref[:, cs] + ob_ref[:, cs]
        ms = jnp.mean(o * o, axis=-1, keepdims=True)
        z = z_ref[:, cs]
        on_scr[:, cs] = (o * lax.rsqrt(ms + EPS) * dnw_ref[...] * (z * _sigmoid(z))).astype(BF16)
    p_a = jnp.dot(on_scr[...], wa_ref[...], preferred_element_type=F32)
    p_b = jnp.dot(sb_ref[...], wb_ref[...], preferred_element_type=F32)
    o_ref[...] = (_sigmoid(ga_ref[...]) * p_a + _sigmoid(gb_ref[...]) * p_b).astype(BF16)


def _merge(o_f, o_b, proj, s_b, dn_norm, w_a, w_b, z_blk, ga_blk, gb_blk):
    n_tok = proj.shape[0]
    wa_in, D = w_a.shape
    wb_in = w_b.shape[0]
    tb = 256
    return pl.pallas_call(
        _merge_kernel,
        grid=(n_tok // tb,),
        in_specs=[pl.BlockSpec((tb, wa_in), lambda i: (i, 0)),
                  pl.BlockSpec((tb, wa_in), lambda i: (i, 0)),
                  pl.BlockSpec((tb, wa_in), lambda i: (i, z_blk)),
                  pl.BlockSpec((tb, wb_in), lambda i: (i, 0)),
                  pl.BlockSpec((tb, D), lambda i: (i, ga_blk)),
                  pl.BlockSpec((tb, D), lambda i: (i, gb_blk)),
                  pl.BlockSpec((1, DN_DK), lambda i: (0, 0)),
                  pl.BlockSpec((wa_in, D), lambda i: (0, 0)),
                  pl.BlockSpec((wb_in, D), lambda i: (0, 0))],
        out_specs=pl.BlockSpec((tb, D), lambda i: (i, 0)),
        out_shape=jax.ShapeDtypeStruct((n_tok, D), BF16),
        scratch_shapes=[pltpu.VMEM((tb, wa_in), BF16)],
        compiler_params=_cparams(("parallel",)),
        name="branch_merge",
    )(o_f, o_b, proj, s_b, proj, proj, dn_norm.reshape(1, DN_DK), w_a.astype(BF16), w_b.astype(BF16))


def _outproj_kernel(m_ref, w_ref, x_ref, gt_ref, o_ref):
    y = jnp.dot(m_ref[...], w_ref[...], preferred_element_type=F32)
    o_ref[...] = x_ref[...] + gt_ref[...] * y


def _out_projection(merged, w_out, x, mod, geom):
    n_tok, D = x.shape
    tb = geom.block(512)
    return pl.pallas_call(
        _outproj_kernel,
        grid=(n_tok // tb,),
        in_specs=[pl.BlockSpec((tb, D), lambda i: (i, 0)),
                  pl.BlockSpec((D, D), lambda i: (0, 0)),
                  pl.BlockSpec((tb, D), lambda i: (i, 0)),
                  _mod_block(geom, tb, 2, D, 1)],
        out_specs=pl.BlockSpec((tb, D), lambda i: (i, 0)),
        out_shape=jax.ShapeDtypeStruct((n_tok, D), F32),
        compiler_params=_cparams(("parallel",)),
        name="out_proj_residual",
    )(merged, w_out.astype(BF16), x, mod)


def _top_rows(cur, k):
    vals = []
    for _ in range(k):
        m = jnp.max(cur, axis=0, keepdims=True)
        vals.append(m)
        cur = jnp.where(cur == m, -jnp.inf, cur)
    return vals


_PAIRS = [(a, b) for a in range(PEER_TOPK) for b in range(PEER_TOPK) if (a + 1) * (b + 1) <= PEER_TOPK]


def _peer_select_kernel(x_ref, nw_ref, sc_ref, sh_ref, wq_ref, keys_ref,
                        h_ref, s0_ref, s1_ref, a_ref, b_ref, th_ref):
    x = x_ref[...]
    ms = jnp.mean(x * x, axis=-1, keepdims=True)
    y = x * lax.rsqrt(ms + EPS) * nw_ref[...]
    h2 = (y * (1.0 + sc_ref[...]) + sh_ref[...]).astype(BF16)
    h_ref[...] = h2
    q_t = lax.dot_general(wq_ref[...], h2, (((1,), (1,)), ((), ())), preferred_element_type=F32)
    dk = PEER_NKEYS
    for h in range(PEER_HEADS):
        svs = []
        for p in range(2):
            r0 = (h * 2 + p) * dk
            s = jnp.dot(keys_ref[h * 2 + p], q_t[r0:r0 + dk, :].astype(BF16), preferred_element_type=F32)
            svs.append(_top_rows(s, PEER_TOPK))
            (s0_ref if p == 0 else s1_ref)[h] = s
        cand = jnp.concatenate([svs[0][a] + svs[1][b] for a, b in _PAIRS], axis=0)
        best = _top_rows(cand, PEER_TOPK)
        m0 = svs[0][0]
        m1 = svs[1][0]
        mx = best[0]
        z = jnp.zeros_like(mx)
        for v in best:
            z = z + jnp.exp(v - mx)
        th_ref[h:h + 1, :] = best[-1]
        a_ref[h] = jnp.exp(s0_ref[h] - m0)
        b_ref[h] = jnp.exp(s1_ref[h] - m1) * (jnp.exp(m0 + m1 - mx) / z)


def _peer_select(x, norm_w, mod, wq_t, keys, geom):
    n_tok, D = x.shape
    tb = geom.block(256)
    H, NK = PEER_HEADS, PEER_NKEYS
    big = jax.ShapeDtypeStruct((H, NK, n_tok), F32)
    big_spec = pl.BlockSpec((H, NK, tb), lambda i: (0, 0, i))
    return pl.pallas_call(
        _peer_select_kernel,
        grid=(n_tok // tb,),
        in_specs=[pl.BlockSpec((tb, D), lambda i: (i, 0)),
                  pl.BlockSpec((1, D), lambda i: (0, 0)),
                  _mod_block(geom, tb, 4, D, 1),
                  _mod_block(geom, tb, 3, D, 1),
                  pl.BlockSpec(wq_t.shape, lambda i: (0, 0)),
                  pl.BlockSpec(keys.shape, lambda i: (0, 0, 0))],
        out_specs=[pl.BlockSpec((tb, D), lambda i: (i, 0)), big_spec, big_spec, big_spec, big_spec,
                   pl.BlockSpec((H, tb), lambda i: (0, i))],
        out_shape=[jax.ShapeDtypeStruct((n_tok, D), BF16), big, big, big, big,
                   jax.ShapeDtypeStruct((H, n_tok), F32)],
        compiler_params=_cparams(("parallel",)),
        name="peer_select",
    )(x, norm_w.reshape(1, D), mod, mod, wq_t, keys)


PEER_EBLK = 1024
PEER_TBLK = 512


def _peer_dense_kernel(h_ref, u_ref, vt_ref, s0_ref, a_ref, s1_ref, b_ref, th_ref, x_ref, gt_ref,
                       o_ref, acc_scr, act_scr, w_scr, *, n_e, tb):
    e = pl.program_id(1)

    @pl.when(e == 0)
    def _():
        acc_scr[...] = jnp.zeros_like(acc_scr)

    act_scr[...] = lax.dot_general(u_ref[...], h_ref[...], (((1,), (1,)), ((), ())), preferred_element_type=F32)
    nk = PEER_NKEYS
    for ii in range(PEER_EBLK // nk):
        rs = slice(ii * nk, (ii + 1) * nk)
        for tc in range(tb // LANE):
            cs = slice(tc * LANE, (tc + 1) * LANE)
            g = jnp.zeros((nk, LANE), F32)
            for h in range(PEER_HEADS):
                t = s1_ref[h, :, cs] + s0_ref[h, ii:ii + 1, cs]
                g = g + jnp.where(t >= th_ref[h:h + 1, cs], b_ref[h, :, cs] * a_ref[h, ii:ii + 1, cs], 0.0)
            w_scr[rs, cs] = (_gelu(act_scr[rs, cs]) * g).astype(BF16)
    acc_scr[...] += jnp.dot(vt_ref[...], w_scr[...], preferred_element_type=F32)

    @pl.when(e == n_e - 1)
    def _():
        o_ref[...] = x_ref[...] + gt_ref[...] * acc_scr[...].T


def _peer_dense(h2, u_bf16, vt_bf16, s0, s1, a, b, th, x, mod, geom):
    n_tok, D = x.shape
    n_exp = u_bf16.shape[0]
    tb = geom.block(PEER_TBLK)
    n_e = n_exp // PEER_EBLK
    H, NK = PEER_HEADS, PEER_NKEYS
    ib = PEER_EBLK // NK
    row = geom.mod_row(tb)
    return pl.pallas_call(
        functools.partial(_peer_dense_kernel, n_e=n_e, tb=tb),
        grid=(n_tok // tb, n_e),
        in_specs=[pl.BlockSpec((tb, D), lambda i, e: (i, 0)),
                  pl.BlockSpec((PEER_EBLK, D), lambda i, e: (e, 0)),
                  pl.BlockSpec((D, PEER_EBLK), lambda i, e: (0, e)),
                  pl.BlockSpec((H, ib, tb), lambda i, e: (0, e, i)),
                  pl.BlockSpec((H, ib, tb), lambda i, e: (0, e, i)),
                  pl.BlockSpec((H, NK, tb), lambda i, e: (0, 0, i)),
                  pl.BlockSpec((H, NK, tb), lambda i, e: (0, 0, i)),
                  pl.BlockSpec((H, tb), lambda i, e: (0, i)),
                  pl.BlockSpec((tb, D), lambda i, e: (i, 0)),
                  pl.BlockSpec((None, None, 1, D), lambda i, e: (row(i), 5, 0, 0))],
        out_specs=pl.BlockSpec((tb, D), lambda i, e: (i, 0)),
        out_shape=jax.ShapeDtypeStruct((n_tok, D), F32),
        scratch_shapes=[pltpu.VMEM((D, tb), F32), pltpu.VMEM((PEER_EBLK, tb), F32),
                        pltpu.VMEM((PEER_EBLK, tb), BF16)],
        compiler_params=_cparams(("parallel", "arbitrary")),
        name="peer_dense",
    )(h2, u_bf16, vt_bf16, s0, a, s1, b, th, x, mod)


def _final_norm_kernel(x_ref, w_ref, o_ref):
    x = x_ref[...]
    ms = jnp.mean(x * x, axis=-1, keepdims=True)
    o_ref[...] = x * lax.rsqrt(ms + EPS) * w_ref[...]


def _final_norm(x, w):
    n_tok, D = x.shape
    tb = _pick(n_tok, 512)
    return pl.pallas_call(
        _final_norm_kernel,
        grid=(n_tok // tb,),
        in_specs=[pl.BlockSpec((tb, D), lambda i: (i, 0)), pl.BlockSpec((1, D), lambda i: (0, 0))],
        out_specs=pl.BlockSpec((tb, D), lambda i: (i, 0)),
        out_shape=jax.ShapeDtypeStruct((n_tok, D), F32),
        compiler_params=_cparams(("parallel",)),
        name="final_norm",
    )(x, w.reshape(1, D))


def _grid_pos_embed(T, D):
    rows = T // GRID_W
    r, col = jnp.meshgrid(jnp.arange(rows, dtype=F32), jnp.arange(GRID_W, dtype=F32), indexing='ij')
    quarter = D // 4
    freq = 1.0 / (10000.0 ** (jnp.arange(quarter, dtype=F32) / quarter))

    def enc(p):
        ang = p.reshape(-1, 1) * freq
        return jnp.concatenate([jnp.sin(ang), jnp.cos(ang)], axis=-1)

    return jnp.concatenate([enc(r), enc(col)], axis=-1)


def _permute_w_in(w_in_l, qk_w, v_w, gm_w, D):
    n_small = 4 * DN_HEADS
    o_small = 2 * qk_w + 2 * v_w
    pad = LANE - n_small
    return jnp.concatenate([w_in_l[:, :o_small], w_in_l[:, o_small + n_small:],
                            w_in_l[:, o_small:o_small + n_small],
                            jnp.zeros((D, pad), w_in_l.dtype)], axis=1).astype(BF16)


def kernel(x_prompt, x_sample, state_delta, c, c_ctx, w_mod, b_mod, norm_mix, w_in, conv_w, dn_a_log,
           dn_dt_bias, dn_norm, gm_norm, gm_w_s, gm_b_s, w_branch_a, w_branch_b, w_out, norm_ffn,
           peer_w_query, peer_sub_keys, peer_u, peer_v, norm_final):
    b_ctx, t_ctx, D = x_prompt.shape
    b_lat, t_lat, _ = x_sample.shape
    depth = w_mod.shape[0]
    geom = _Geom(b_ctx, t_ctx, b_lat, t_lat)
    qk_w = DN_HEADS * DN_DK
    v_w = qk_w
    gm_w = GM_GROUPS * GM_CH
    assert t_ctx % CONV_BLK == 0 and t_lat % CONV_BLK == 0 and qk_w == gm_w and D == 2 * qk_w

    xs = x_sample + _grid_pos_embed(t_lat, D).astype(x_sample.dtype)[None]
    x = jnp.concatenate([x_prompt.reshape(geom.n_ctx, D), xs.reshape(geom.n_lat, D)], axis=0)

    n_rows = -(-(1 + b_lat) // SUBLANE) * SUBLANE
    cond = jnp.concatenate([c_ctx[None, :], c, jnp.zeros((n_rows - 1 - b_lat, D), c.dtype)], axis=0)
    mod_all = _modulation(cond, w_mod, b_mod).reshape(depth, n_rows, N_MOD, 1, D)

    blk1k = qk_w
    z_blk, u_blk, vg_blk = 3, 4, 5
    ga_blk, gb_blk = 3, 4
    small_blk = (3 * qk_w + v_w + 2 * gm_w + 2 * D) // LANE

    zeros_state = jnp.zeros((b_ctx, DN_HEADS, DN_DK, DN_DK), F32)
    new_states = []
    for l in range(depth):
        mod = mod_all[l]
        w_in_l = _permute_w_in(w_in[l], qk_w, v_w, gm_w, D)
        proj = _in_projection(x, norm_mix[l], mod, w_in_l, geom)
        qkv = _conv_qkv(proj, conv_w[l], geom)
        outs = []
        states = []
        for d, reverse in ((0, False), (1, True)):
            o_c, s_c = _deltanet_dir(qkv, proj, zeros_state, dn_a_log[l, d], dn_dt_bias[l, d], reverse=reverse,
                                     d=d, tok_off=0, n_seq=b_ctx, t_seq=t_ctx, small_blk=small_blk)
            o_l, _ = _deltanet_dir(qkv, proj, state_delta[:, l, d], dn_a_log[l, d], dn_dt_bias[l, d],
                                   reverse=reverse, d=d, tok_off=geom.n_ctx, n_seq=b_lat, t_seq=t_lat,
                                   small_blk=small_blk)
            outs.append(jnp.concatenate([o_c, o_l], axis=0))
            states.append(s_c)
        new_states.append(jnp.stack(states, axis=1))
        s_b = _gmlp(proj, gm_norm[l], gm_w_s[l], gm_b_s[l], u_blk, vg_blk)
        merged = _merge(outs[0], outs[1], proj, s_b, dn_norm[l], w_branch_a[l], w_branch_b[l], z_blk, ga_blk, gb_blk)
        x = _out_projection(merged, w_out[l], x, mod, geom)
        wq_t = peer_w_query[l].reshape(D, -1).T.astype(BF16)
        keys = peer_sub_keys[l].reshape(PEER_HEADS * 2, PEER_NKEYS, -1).astype(BF16)
        h2, s0, s1, a, b, th = _peer_select(x, norm_ffn[l], mod, wq_t, keys, geom)
        x = _peer_dense(h2, peer_u[l].astype(BF16), peer_v[l].T.astype(BF16), s0, s1, a, b, th, x, mod, geom)

    y = _final_norm(x, norm_final)
    y_prompt = y[:geom.n_ctx].reshape(b_ctx, t_ctx, D)
    y_sample = y[geom.n_ctx:].reshape(b_lat, t_lat, D)
    new_state_delta = jnp.stack(new_states, axis=1).astype(x_prompt.dtype)
    return (y_prompt, y_sample, new_state_delta)
```

```python
import functools
import math

import numpy as np
import jax
import jax.numpy as jnp
from jax import lax
from jax.experimental import pallas as pl
from jax.experimental.pallas import tpu as pltpu

F32 = jnp.float32
BF16 = jnp.bfloat16

EPS = 1e-6
N_MOD = 6
GRID_W = 64
DN_HEADS = 8
DN_DK = 128
DN_CHUNK = 256
GM_GROUPS = 8
GM_CH = 128
GM_CHUNK = 128
PEER_HEADS = 8
PEER_NKEYS = 128
PEER_TOPK = 16
LANE = 128
SUBLANE = 8
VMEM_LIMIT = 56 * 1024 * 1024
NEG_BIG = -1e30

_HI = lax.Precision.HIGHEST


def _cparams(sem):
    return pltpu.CompilerParams(dimension_semantics=sem, vmem_limit_bytes=VMEM_LIMIT)


def _pick(n, cap):
    b = 1
    while b * 2 <= cap and n % (b * 2) == 0:
        b *= 2
    return b


def _dot(a, b):
    return jnp.dot(a.astype(BF16), b.astype(BF16), preferred_element_type=F32)


def _dot_nt(a, b):
    return lax.dot_general(a.astype(BF16), b.astype(BF16), (((1,), (1,)), ((), ())),
                           preferred_element_type=F32)


def _dot_tn(a, b):
    return lax.dot_general(a.astype(BF16), b.astype(BF16), (((0,), (0,)), ((), ())),
                           preferred_element_type=F32)


def _gelu(x):
    return 0.5 * x * (1.0 + jnp.tanh(math.sqrt(2.0 / math.pi) * (x + 0.044715 * (x * x * x))))


def _sigmoid(x):
    return 1.0 / (1.0 + jnp.exp(-x))


def _softplus(x):
    return jnp.maximum(x, 0.0) + jnp.log(1.0 + jnp.exp(-jnp.abs(x)))


def _mod_kernel(c_ref, w_ref, b_ref, o_ref):
    c = c_ref[...]
    o_ref[...] = _dot(c * _sigmoid(c), w_ref[...]) + b_ref[...]


def _modulation(cond, w_mod, b_mod):
    L, D, N = w_mod.shape
    R = cond.shape[0]
    tn = _pick(N, 1024)
    return pl.pallas_call(
        _mod_kernel,
        grid=(L, N // tn),
        in_specs=[pl.BlockSpec((R, D), lambda l, j: (0, 0)),
                  pl.BlockSpec((None, D, tn), lambda l, j: (l, 0, j)),
                  pl.BlockSpec((None, 1, tn), lambda l, j: (l, 0, j))],
        out_specs=pl.BlockSpec((None, R, tn), lambda l, j: (l, 0, j)),
        out_shape=jax.ShapeDtypeStruct((L, R, N), F32),
        compiler_params=_cparams(("parallel", "parallel")),
        name="adaln_mod",
    )(cond, w_mod, b_mod.reshape(L, 1, N))


class _Geom:
    def __init__(self, b_ctx, t_ctx, b_lat, t_lat):
        self.b_ctx, self.t_ctx, self.b_lat, self.t_lat = b_ctx, t_ctx, b_lat, t_lat
        self.n_ctx = b_ctx * t_ctx
        self.n_lat = b_lat * t_lat
        self.n_tok = self.n_ctx + self.n_lat

    def block(self, cap):
        return _pick(math.gcd(self.n_ctx, self.t_lat), cap)

    def mod_row(self, tb):
        n_ctx_blocks = self.n_ctx // tb
        per_seq = self.t_lat // tb

        def row(i):
            return jnp.where(i < n_ctx_blocks, 0, 1 + (i - n_ctx_blocks) // per_seq)

        return row


def _inproj_kernel(x_ref, nw_ref, sc_ref, sh_ref, w_ref, o_ref, h_scr):
    @pl.when(pl.program_id(1) == 0)
    def _():
        x = x_ref[...]
        ms = jnp.mean(x * x, axis=-1, keepdims=True)
        y = x * lax.rsqrt(ms + EPS) * nw_ref[...]
        h_scr[...] = (y * (1.0 + sc_ref[...]) + sh_ref[...]).astype(BF16)

    o_ref[...] = jnp.dot(h_scr[...], w_ref[...], preferred_element_type=F32)


def _mod_block(geom, tb, which, D, ngrid):
    row = geom.mod_row(tb)
    if ngrid == 1:
        return pl.BlockSpec((None, None, 1, D), lambda i: (row(i), which, 0, 0))
    return pl.BlockSpec((None, None, 1, D), lambda i, j: (row(i), which, 0, 0))


def _in_projection(x, norm_w, mod, w_bf16, geom):
    n_tok, D = x.shape
    N = w_bf16.shape[1]
    tb = geom.block(1024)
    tn = 1152 if N % 1152 == 0 else _pick(N, 1024)
    return pl.pallas_call(
        _inproj_kernel,
        grid=(n_tok // tb, N // tn),
        in_specs=[pl.BlockSpec((tb, D), lambda i, j: (i, 0)),
                  pl.BlockSpec((1, D), lambda i, j: (0, 0)),
                  _mod_block(geom, tb, 1, D, 2),
                  _mod_block(geom, tb, 0, D, 2),
                  pl.BlockSpec((D, tn), lambda i, j: (0, j))],
        out_specs=pl.BlockSpec((tb, tn), lambda i, j: (i, j)),
        out_shape=jax.ShapeDtypeStruct((n_tok, N), F32),
        scratch_shapes=[pltpu.VMEM((tb, D), BF16)],
        compiler_params=_cparams(("parallel", "arbitrary")),
        name="norm_in_proj",
    )(x, norm_w.reshape(1, D), mod, mod, w_bf16)


CONV_BLK = 256


def _conv_kernel(cur_ref, prev_ref, next_ref, w_ref, o_ref, *, geom):
    i = pl.program_id(0)
    j = pl.program_id(1)
    n_ctx_blocks = geom.n_ctx // CONV_BLK
    bps_ctx = geom.t_ctx // CONV_BLK
    bps_lat = geom.t_lat // CONV_BLK
    pos = jnp.where(i < n_ctx_blocks, i % bps_ctx, (i - n_ctx_blocks) % bps_lat)
    per = jnp.where(i < n_ctx_blocks, bps_ctx, bps_lat)
    is_start = pos == 0
    is_end = pos == per - 1

    x = cur_ref[...]
    prev_row = jnp.where(is_start, 0.0, prev_ref[SUBLANE - 1:SUBLANE, :])
    next_row = jnp.where(is_end, 0.0, next_ref[0:1, :])
    rid = lax.broadcasted_iota(jnp.int32, x.shape, 0)
    xm1 = jnp.where(rid == 0, prev_row, pltpu.roll(x, 1, axis=0))
    xp1 = jnp.where(rid == CONV_BLK - 1, next_row, pltpu.roll(x, CONV_BLK - 1, axis=0))
    w = w_ref[...]
    y = w[0:1, :] * xm1 + w[1:2, :] * x + w[2:3, :] * xp1
    y = y * _sigmoid(y)

    @pl.when(j == 2)
    def _():
        o_ref[...] = y

    @pl.when(j < 2)
    def _():
        scale = jnp.where(j == 0, DN_DK ** -0.5, 1.0).astype(F32)
        for h in range(DN_HEADS):
            seg = y[:, h * DN_DK:(h + 1) * DN_DK]
            ss = jnp.sum(seg * seg, axis=-1, keepdims=True)
            o_ref[:, h * DN_DK:(h + 1) * DN_DK] = seg * (lax.rsqrt(ss + EPS) * scale)


def _conv_qkv(proj, conv_w, geom):
    n_tok = proj.shape[0]
    width = DN_HEADS * DN_DK
    nb8 = n_tok // SUBLANE
    r = CONV_BLK // SUBLANE
    return pl.pallas_call(
        functools.partial(_conv_kernel, geom=geom),
        grid=(n_tok // CONV_BLK, 3),
        in_specs=[pl.BlockSpec((CONV_BLK, width), lambda i, j: (i, j)),
                  pl.BlockSpec((SUBLANE, width), lambda i, j: (jnp.maximum(i * r - 1, 0), j)),
                  pl.BlockSpec((SUBLANE, width), lambda i, j: (jnp.minimum((i + 1) * r, nb8 - 1), j)),
                  pl.BlockSpec((3, width), lambda i, j: (0, j))],
        out_specs=pl.BlockSpec((CONV_BLK, width), lambda i, j: (i, j)),
        out_shape=jax.ShapeDtypeStruct((n_tok, 3 * width), F32),
        compiler_params=_cparams(("parallel", "parallel")),
        name="conv_silu_l2",
    )(proj, proj, proj, conv_w)


DN_WIN = 256
_LEVELS = tuple(1 << i for i in range(DN_CHUNK.bit_length() - 1))


def _dn_constants(reverse):
    n = DN_CHUNK
    r = np.arange(n)[:, None]
    c = np.arange(n)[None, :]
    before = (c > r) if reverse else (c < r)
    incl = before | (r == c)
    lvl = []
    for s in _LEVELS:
        same = (r // (2 * s)) == (c // (2 * s))
        if reverse:
            m = same & ((r % (2 * s)) < s) & ((c % (2 * s)) >= s)
        else:
            m = same & ((r % (2 * s)) >= s) & ((c % (2 * s)) < s)
        lvl.append(m)
    masks = np.stack([before, incl] + lvl).astype(np.float32)
    cum_col = incl.astype(np.float32)
    return jnp.asarray(masks), jnp.asarray(cum_col), jnp.asarray(cum_col.T)


def _dn_kernel(q_ref, k_ref, v_ref, sm_ref, s0_ref, pa_ref, pdt_ref, msk_ref, cc_ref, cr_ref,
               o_ref, sfin_ref, s_scr, *, reverse, d, n_win):
    wi = pl.program_id(1)

    @pl.when(wi == 0)
    def _():
        s_scr[...] = s0_ref[...]

    sm = sm_ref[...]
    g_all = pa_ref[...] * _softplus(sm + pdt_ref[...])
    beta_all = _sigmoid(sm)
    g_t = g_all.T
    before = msk_ref[0]
    incl = msk_ref[1]
    eye = incl - before
    cum_col = cc_ref[...]
    cum_row = cr_ref[...]
    n_chunks = DN_WIN // DN_CHUNK
    order = range(n_chunks - 1, -1, -1) if reverse else range(n_chunks)
    last = 0 if reverse else DN_CHUNK - 1
    for c in order:
        r0 = c * DN_CHUNK
        gc_col = jnp.dot(cum_col, g_all[r0:r0 + DN_CHUNK, :], precision=_HI, preferred_element_type=F32)
        gc_row = jnp.dot(g_t[:, r0:r0 + DN_CHUNK], cum_row, precision=_HI, preferred_element_type=F32)
        for h in range(DN_HEADS):
            la = 16 + 8 * d + h
            lb = 8 * d + h
            gcc = gc_col[:, la:la + 1]
            gcr = gc_row[la:la + 1, :]
            bcol = beta_all[r0:r0 + DN_CHUNK, lb:lb + 1]
            dec = jnp.exp(jnp.where(incl > 0, gcc - gcr, NEG_BIG))
            cs = slice(h * DN_DK, (h + 1) * DN_DK)
            qh = q_ref[r0:r0 + DN_CHUNK, cs]
            kh = k_ref[r0:r0 + DN_CHUNK, cs]
            vh = v_ref[r0:r0 + DN_CHUNK, cs]
            kk = _dot_nt(kh, kh)
            qk = _dot_nt(qh, kh)
            m = (bcol * kk) * (dec * before)
            t = eye - m * msk_ref[2]
            for li in range(1, len(_LEVELS)):
                cl = m * msk_ref[2 + li]
                t = t - _dot(_dot(t, cl), t)
            egc = jnp.exp(gcc)
            rhs = jnp.concatenate([vh * bcol, kh * (bcol * egc)], axis=1)
            sol = _dot(t, rhs)
            u = sol[:, :DN_DK]
            w = sol[:, DN_DK:]
            s = s_scr[h]
            v_new = u - _dot(w, s)
            o_ref[r0:r0 + DN_CHUNK, cs] = _dot(qh * egc, s) + _dot(qk * dec, v_new)
            gl = gcc[last:last + 1, :]
            s_scr[h] = s * jnp.exp(gl) + _dot_tn(kh * jnp.exp(gl - gcc), v_new)

    @pl.when(wi == n_win - 1)
    def _():
        sfin_ref[...] = s_scr[...]


def _deltanet_dir(qkv, proj, s0, a_log, dt_bias, *, reverse, d, tok_off, n_seq, t_seq, small_blk):
    n_tok = qkv.shape[0]
    width = DN_HEADS * DN_DK
    n_win = t_seq // DN_WIN
    blk0 = tok_off // DN_WIN
    masks, cum_col, cum_row = _dn_constants(reverse)
    lanes = jnp.arange(LANE)
    hd = jnp.clip(lanes - (16 + 8 * d), 0, DN_HEADS - 1)
    valid = (lanes >= 16 + 8 * d) & (lanes < 24 + 8 * d)
    pa = jnp.where(valid, -jnp.exp(a_log.astype(F32))[hd], 0.0).reshape(1, LANE)
    pdt = jnp.where(valid, dt_bias.astype(F32)[hd], 0.0).reshape(1, LANE)

    def tok_blk(b, w):
        w_eff = (n_win - 1 - w) if reverse else w
        return blk0 + b * n_win + w_eff

    dk2 = DN_DK
    return pl.pallas_call(
        functools.partial(_dn_kernel, reverse=reverse, d=d, n_win=n_win),
        grid=(n_seq, n_win),
        in_specs=[pl.BlockSpec((DN_WIN, width), lambda b, w: (tok_blk(b, w), 0)),
                  pl.BlockSpec((DN_WIN, width), lambda b, w: (tok_blk(b, w), 1)),
                  pl.BlockSpec((DN_WIN, width), lambda b, w: (tok_blk(b, w), 2)),
                  pl.BlockSpec((DN_WIN, LANE), lambda b, w: (tok_blk(b, w), small_blk)),
                  pl.BlockSpec((None, DN_HEADS, dk2, dk2), lambda b, w: (b, 0, 0, 0)),
                  pl.BlockSpec((1, LANE), lambda b, w: (0, 0)),
                  pl.BlockSpec((1, LANE), lambda b, w: (0, 0)),
                  pl.BlockSpec(masks.shape, lambda b, w: (0, 0, 0)),
                  pl.BlockSpec(cum_col.shape, lambda b, w: (0, 0)),
                  pl.BlockSpec(cum_row.shape, lambda b, w: (0, 0))],
        out_specs=[pl.BlockSpec((DN_WIN, width), lambda b, w: (tok_blk(b, w) - blk0, 0)),
                   pl.BlockSpec((None, DN_HEADS, dk2, dk2), lambda b, w: (b, 0, 0, 0))],
        out_shape=[jax.ShapeDtypeStruct((n_seq * t_seq, width), F32),
                   jax.ShapeDtypeStruct((n_seq, DN_HEADS, dk2, dk2), F32)],
        scratch_shapes=[pltpu.VMEM((DN_HEADS, dk2, dk2), F32)],
        compiler_params=_cparams(("parallel", "arbitrary")),
        name=f"deltanet_{'bwd' if reverse else 'fwd'}",
    )(qkv, qkv, qkv, proj, s0, pa, pdt, masks, cum_col, cum_row)


def _gmlp_kernel(u_ref, vg_ref, nw_ref, ws_ref, bias_ref, o_ref, *, tb):
    vg = _gelu(vg_ref[...])
    ms = jnp.mean(vg * vg, axis=-1, keepdims=True)
    vn = (vg * lax.rsqrt(ms + EPS) * nw_ref[...]).astype(BF16)
    for ch in range(tb // GM_CHUNK):
        rs = slice(ch * GM_CHUNK, (ch + 1) * GM_CHUNK)
        for g in range(GM_GROUPS):
            cs = slice(g * GM_CH, (g + 1) * GM_CH)
            mixed = jnp.dot(ws_ref[g], vn[rs, cs], preferred_element_type=F32) + bias_ref[g]
            o_ref[rs, cs] = (_gelu(u_ref[rs, cs]) * mixed).astype(BF16)


def _gmlp(proj, gm_norm, w_s, b_s, u_blk, vg_blk):
    n_tok = proj.shape[0]
    width = GM_GROUPS * GM_CH
    tb = 256
    bias = jnp.broadcast_to(b_s.astype(F32)[:, :, None], (GM_GROUPS, GM_CHUNK, GM_CH))
    return pl.pallas_call(
        functools.partial(_gmlp_kernel, tb=tb),
        grid=(n_tok // tb,),
        in_specs=[pl.BlockSpec((tb, width), lambda i: (i, u_blk)),
                  pl.BlockSpec((tb, width), lambda i: (i, vg_blk)),
                  pl.BlockSpec((1, width), lambda i: (0, 0)),
                  pl.BlockSpec((GM_GROUPS, GM_CHUNK, GM_CHUNK), lambda i: (0, 0, 0)),
                  pl.BlockSpec((GM_GROUPS, GM_CHUNK, GM_CH), lambda i: (0, 0, 0))],
        out_specs=pl.BlockSpec((tb, width), lambda i: (i, 0)),
        out_shape=jax.ShapeDtypeStruct((n_tok, width), BF16),
        compiler_params=_cparams(("parallel",)),
        name="gmlp_gate",
    )(proj, proj, gm_norm.reshape(1, width), w_s.astype(BF16), bias)


def _merge_kernel(of_ref, ob_ref, z_ref, sb_ref, ga_ref, gb_ref, dnw_ref, wa_ref, wb_ref, o_ref, on_scr):
    for h in range(DN_HEADS):
        cs = slice(h * DN_DK, (h + 1) * DN_DK)
        o = of_ref[:, cs] + ob_ref[:, cs]
        ms = jnp.mean(o * o, axis=-1, keepdims=True)
        z = z_ref[:, cs]
        on_scr[:, cs] = (o * lax.rsqrt(ms + EPS) * dnw_ref[...] * (z * _sigmoid(z))).astype(BF16)
    p_a = jnp.dot(on_scr[...], wa_ref[...], preferred_element_type=F32)
    p_b = jnp.dot(sb_ref[...], wb_ref[...], preferred_element_type=F32)
    o_ref[...] = (_sigmoid(ga_ref[...]) * p_a + _sigmoid(gb_ref[...]) * p_b).astype(BF16)


def _merge(o_f, o_b, proj, s_b, dn_norm, w_a, w_b, z_blk, ga_blk, gb_blk):
    n_tok = proj.shape[0]
    wa_in, D = w_a.shape
    wb_in = w_b.shape[0]
    tb = 256
    return pl.pallas_call(
        _merge_kernel,
        grid=(n_tok // tb,),
        in_specs=[pl.BlockSpec((tb, wa_in), lambda i: (i, 0)),
                  pl.BlockSpec((tb, wa_in), lambda i: (i, 0)),
                  pl.BlockSpec((tb, wa_in), lambda i: (i, z_blk)),
                  pl.BlockSpec((tb, wb_in), lambda i: (i, 0)),
                  pl.BlockSpec((tb, D), lambda i: (i, ga_blk)),
                  pl.BlockSpec((tb, D), lambda i: (i, gb_blk)),
                  pl.BlockSpec((1, DN_DK), lambda i: (0, 0)),
                  pl.BlockSpec((wa_in, D), lambda i: (0, 0)),
                  pl.BlockSpec((wb_in, D), lambda i: (0, 0))],
        out_specs=pl.BlockSpec((tb, D), lambda i: (i, 0)),
        out_shape=jax.ShapeDtypeStruct((n_tok, D), BF16),
        scratch_shapes=[pltpu.VMEM((tb, wa_in), BF16)],
        compiler_params=_cparams(("parallel",)),
        name="branch_merge",
    )(o_f, o_b, proj, s_b, proj, proj, dn_norm.reshape(1, DN_DK), w_a.astype(BF16), w_b.astype(BF16))


def _outproj_kernel(m_ref, w_ref, x_ref, gt_ref, o_ref):
    y = jnp.dot(m_ref[...], w_ref[...], preferred_element_type=F32)
    o_ref[...] = x_ref[...] + gt_ref[...] * y


def _out_projection(merged, w_out, x, mod, geom):
    n_tok, D = x.shape
    tb = geom.block(512)
    return pl.pallas_call(
        _outproj_kernel,
        grid=(n_tok // tb,),
        in_specs=[pl.BlockSpec((tb, D), lambda i: (i, 0)),
                  pl.BlockSpec((D, D), lambda i: (0, 0)),
                  pl.BlockSpec((tb, D), lambda i: (i, 0)),
                  _mod_block(geom, tb, 2, D, 1)],
        out_specs=pl.BlockSpec((tb, D), lambda i: (i, 0)),
        out_shape=jax.ShapeDtypeStruct((n_tok, D), F32),
        compiler_params=_cparams(("parallel",)),
        name="out_proj_residual",
    )(merged, w_out.astype(BF16), x, mod)


def _top_rows(cur, k):
    vals = []
    for _ in range(k):
        m = jnp.max(cur, axis=0, keepdims=True)
        vals.append(m)
        cur = jnp.where(cur == m, -jnp.inf, cur)
    return vals


_PAIRS = [(a, b) for a in range(PEER_TOPK) for b in range(PEER_TOPK) if (a + 1) * (b + 1) <= PEER_TOPK]


def _peer_select_kernel(x_ref, nw_ref, sc_ref, sh_ref, wq_ref, keys_ref,
                        ht_ref, s0_ref, s1_ref, a_ref, b_ref, th_ref):
    x = x_ref[...]
    ms = jnp.mean(x * x, axis=-1, keepdims=True)
    y = x * lax.rsqrt(ms + EPS) * nw_ref[...]
    h2_t = (y * (1.0 + sc_ref[...]) + sh_ref[...]).T.astype(BF16)
    ht_ref[...] = h2_t
    q_t = jnp.dot(wq_ref[...], h2_t, preferred_element_type=F32)
    dk = PEER_NKEYS
    for h in range(PEER_HEADS):
        svs = []
        scores = []
        for p in range(2):
            r0 = (h * 2 + p) * dk
            s = jnp.dot(keys_ref[h * 2 + p], q_t[r0:r0 + dk, :].astype(BF16), preferred_element_type=F32)
            svs.append(_top_rows(s, PEER_TOPK))
            scores.append(s)
        cand = jnp.concatenate([svs[0][a] + svs[1][b] for a, b in _PAIRS], axis=0)
        best = _top_rows(cand, PEER_TOPK)
        m0 = svs[0][0]
        m1 = svs[1][0]
        mx = best[0]
        z = jnp.zeros_like(mx)
        for v in best:
            z = z + jnp.exp(v - mx)
        s0_ref[h] = scores[0]
        s1_ref[h] = scores[1]
        th_ref[h:h + 1, :] = best[-1]
        a_ref[h] = jnp.exp(scores[0] - m0)
        b_ref[h] = jnp.exp(scores[1] - m1) * (0.5 * jnp.exp(m0 + m1 - mx) / z)


def _peer_select(x, norm_w, mod, wq_t, keys, geom):
    n_tok, D = x.shape
    tb = geom.block(256)
    H, NK = PEER_HEADS, PEER_NKEYS
    big = jax.ShapeDtypeStruct((H, NK, n_tok), F32)
    big_spec = pl.BlockSpec((H, NK, tb), lambda i: (0, 0, i))
    return pl.pallas_call(
        _peer_select_kernel,
        grid=(n_tok // tb,),
        in_specs=[pl.BlockSpec((tb, D), lambda i: (i, 0)),
                  pl.BlockSpec((1, D), lambda i: (0, 0)),
                  _mod_block(geom, tb, 4, D, 1),
                  _mod_block(geom, tb, 3, D, 1),
                  pl.BlockSpec(wq_t.shape, lambda i: (0, 0)),
                  pl.BlockSpec(keys.shape, lambda i: (0, 0, 0))],
        out_specs=[pl.BlockSpec((D, tb), lambda i: (0, i)), big_spec, big_spec, big_spec, big_spec,
                   pl.BlockSpec((H, tb), lambda i: (0, i))],
        out_shape=[jax.ShapeDtypeStruct((D, n_tok), BF16), big, big, big, big,
                   jax.ShapeDtypeStruct((H, n_tok), F32)],
        compiler_params=_cparams(("parallel",)),
        name="peer_select",
    )(x, norm_w.reshape(1, D), mod, mod, wq_t, keys)


PEER_EBLK = 1024
PEER_TBLK = 512


_PD_ROWS = 64
_PD_IGRP = 4
_GELU_C1 = math.sqrt(2.0 / math.pi)
_GELU_C2 = _GELU_C1 * 0.044715


def _peer_dense_kernel(ht_ref, u_ref, vt_ref, s0_ref, a_ref, s1_ref, b_ref, th_ref, x_ref, gt_ref,
                       o_ref, acc_scr, act0, act1, w0, w1, *, n_e, tb):
    s = pl.program_id(0)
    e_c = jnp.maximum(s - 2, 0) % n_e

    @pl.when(s == 0)
    def _():
        for r in (act0, act1, w0, w1):
            r[...] = jnp.zeros_like(r)

    @pl.when(e_c == 0)
    def _():
        acc_scr[...] = jnp.zeros_like(acc_scr)

    def stages(act_w, act_r, w_w, w_r):
        act_w[...] = jnp.dot(u_ref[...], ht_ref[...], preferred_element_type=F32)
        acc_scr[...] += jnp.dot(vt_ref[...], w_r[...], preferred_element_type=F32)
        nk = PEER_NKEYS
        for tc in range(tb // LANE):
            cs = slice(tc * LANE, (tc + 1) * LANE)
            for jh in range(nk // _PD_ROWS):
                js = slice(jh * _PD_ROWS, (jh + 1) * _PD_ROWS)
                for ig in range(PEER_EBLK // nk // _PD_IGRP):
                    g = [jnp.zeros((_PD_ROWS, LANE), F32) for _ in range(_PD_IGRP)]
                    for h in range(PEER_HEADS):
                        s1t = s1_ref[h, js, cs]
                        bt = b_ref[h, js, cs]
                        thr = th_ref[h:h + 1, cs]
                        for k in range(_PD_IGRP):
                            ii = ig * _PD_IGRP + k
                            t = s1t + s0_ref[h, ii:ii + 1, cs]
                            g[k] = g[k] + jnp.where(t >= thr, bt * a_ref[h, ii:ii + 1, cs], 0.0)
                    for k in range(_PD_IGRP):
                        r0 = (ig * _PD_IGRP + k) * nk + jh * _PD_ROWS
                        xa = act_r[r0:r0 + _PD_ROWS, cs]
                        tt = jnp.tanh(xa * (_GELU_C1 + _GELU_C2 * (xa * xa)))
                        w_w[r0:r0 + _PD_ROWS, cs] = ((xa + xa * tt) * g[k]).astype(BF16)

    @pl.when(s % 2 == 0)
    def _():
        stages(act0, act1, w1, w0)

    @pl.when(s % 2 == 1)
    def _():
        stages(act1, act0, w0, w1)

    @pl.when((e_c == n_e - 1) & (s >= 2))
    def _():
        o_ref[...] = x_ref[...] + gt_ref[...] * acc_scr[...].T


def _peer_dense(h2_t, u_bf16, vt_bf16, s0, s1, a, b_half, th, x, mod, geom):
    n_tok, D = x.shape
    n_exp = u_bf16.shape[0]
    tb = geom.block(PEER_TBLK)
    n_e = n_exp // PEER_EBLK
    n_steps = (n_tok // tb) * n_e
    H, NK = PEER_HEADS, PEER_NKEYS
    ib = PEER_EBLK // NK
    row = geom.mod_row(tb)
    one = pl.Buffered(1)

    def step(s, lag):
        return jnp.clip(s - lag, 0, n_steps - 1)

    def tok(s, lag):
        return step(s, lag) // n_e

    def exp(s, lag):
        return step(s, lag) % n_e

    return pl.pallas_call(
        functools.partial(_peer_dense_kernel, n_e=n_e, tb=tb),
        grid=(n_steps + 2,),
        in_specs=[pl.BlockSpec((D, tb), lambda s: (0, tok(s, 0))),
                  pl.BlockSpec((PEER_EBLK, D), lambda s: (exp(s, 0), 0)),
                  pl.BlockSpec((D, PEER_EBLK), lambda s: (0, exp(s, 2))),
                  pl.BlockSpec((H, ib, tb), lambda s: (0, exp(s, 1), tok(s, 1))),
                  pl.BlockSpec((H, ib, tb), lambda s: (0, exp(s, 1), tok(s, 1))),
                  pl.BlockSpec((H, NK, tb), lambda s: (0, 0, tok(s, 1)), pipeline_mode=one),
                  pl.BlockSpec((H, NK, tb), lambda s: (0, 0, tok(s, 1)), pipeline_mode=one),
                  pl.BlockSpec((H, tb), lambda s: (0, tok(s, 1))),
                  pl.BlockSpec((tb, D), lambda s: (tok(s, 2), 0), pipeline_mode=one),
                  pl.BlockSpec((None, None, 1, D), lambda s: (row(tok(s, 2)), 5, 0, 0))],
        out_specs=pl.BlockSpec((tb, D), lambda s: (tok(s, 2), 0)),
        out_shape=jax.ShapeDtypeStruct((n_tok, D), F32),
        scratch_shapes=[pltpu.VMEM((D, tb), F32),
                        pltpu.VMEM((PEER_EBLK, tb), F32), pltpu.VMEM((PEER_EBLK, tb), F32),
                        pltpu.VMEM((PEER_EBLK, tb), BF16), pltpu.VMEM((PEER_EBLK, tb), BF16)],
        compiler_params=_cparams(("arbitrary",)),
        name="peer_dense",
    )(h2_t, u_bf16, vt_bf16, s0, a, s1, b_half, th, x, mod)


def _final_norm_kernel(x_ref, w_ref, o_ref):
    x = x_ref[...]
    ms = jnp.mean(x * x, axis=-1, keepdims=True)
    o_ref[...] = x * lax.rsqrt(ms + EPS) * w_ref[...]


def _final_norm(x, w):
    n_tok, D = x.shape
    tb = _pick(n_tok, 512)
    return pl.pallas_call(
        _final_norm_kernel,
        grid=(n_tok // tb,),
        in_specs=[pl.BlockSpec((tb, D), lambda i: (i, 0)), pl.BlockSpec((1, D), lambda i: (0, 0))],
        out_specs=pl.BlockSpec((tb, D), lambda i: (i, 0)),
        out_shape=jax.ShapeDtypeStruct((n_tok, D), F32),
        compiler_params=_cparams(("parallel",)),
        name="final_norm",
    )(x, w.reshape(1, D))


def _grid_pos_embed(T, D):
    rows = T // GRID_W
    r, col = jnp.meshgrid(jnp.arange(rows, dtype=F32), jnp.arange(GRID_W, dtype=F32), indexing='ij')
    quarter = D // 4
    freq = 1.0 / (10000.0 ** (jnp.arange(quarter, dtype=F32) / quarter))

    def enc(p):
        ang = p.reshape(-1, 1) * freq
        return jnp.concatenate([jnp.sin(ang), jnp.cos(ang)], axis=-1)

    return jnp.concatenate([enc(r), enc(col)], axis=-1)


def _permute_w_in(w_in_l, qk_w, v_w, gm_w, D):
    n_small = 4 * DN_HEADS
    o_small = 2 * qk_w + 2 * v_w
    pad = LANE - n_small
    return jnp.concatenate([w_in_l[:, :o_small], w_in_l[:, o_small + n_small:],
                            w_in_l[:, o_small:o_small + n_small],
                            jnp.zeros((D, pad), w_in_l.dtype)], axis=1).astype(BF16)


def kernel(x_prompt, x_sample, state_delta, c, c_ctx, w_mod, b_mod, norm_mix, w_in, conv_w, dn_a_log,
           dn_dt_bias, dn_norm, gm_norm, gm_w_s, gm_b_s, w_branch_a, w_branch_b, w_out, norm_ffn,
           peer_w_query, peer_sub_keys, peer_u, peer_v, norm_final):
    b_ctx, t_ctx, D = x_prompt.shape
    b_lat, t_lat, _ = x_sample.shape
    depth = w_mod.shape[0]
    geom = _Geom(b_ctx, t_ctx, b_lat, t_lat)
    qk_w = DN_HEADS * DN_DK
    v_w = qk_w
    gm_w = GM_GROUPS * GM_CH
    assert t_ctx % CONV_BLK == 0 and t_lat % CONV_BLK == 0 and qk_w == gm_w and D == 2 * qk_w

    xs = x_sample + _grid_pos_embed(t_lat, D).astype(x_sample.dtype)[None]
    x = jnp.concatenate([x_prompt.reshape(geom.n_ctx, D), xs.reshape(geom.n_lat, D)], axis=0)

    n_rows = -(-(1 + b_lat) // SUBLANE) * SUBLANE
    cond = jnp.concatenate([c_ctx[None, :], c, jnp.zeros((n_rows - 1 - b_lat, D), c.dtype)], axis=0)
    mod_all = _modulation(cond, w_mod, b_mod).reshape(depth, n_rows, N_MOD, 1, D)

    blk1k = qk_w
    z_blk, u_blk, vg_blk = 3, 4, 5
    ga_blk, gb_blk = 3, 4
    small_blk = (3 * qk_w + v_w + 2 * gm_w + 2 * D) // LANE

    zeros_state = jnp.zeros((b_ctx, DN_HEADS, DN_DK, DN_DK), F32)
    new_states = []
    for l in range(depth):
        mod = mod_all[l]
        w_in_l = _permute_w_in(w_in[l], qk_w, v_w, gm_w, D)
        proj = _in_projection(x, norm_mix[l], mod, w_in_l, geom)
        qkv = _conv_qkv(proj, conv_w[l], geom)
        outs = []
        states = []
        for d, reverse in ((0, False), (1, True)):
            o_c, s_c = _deltanet_dir(qkv, proj, zeros_state, dn_a_log[l, d], dn_dt_bias[l, d], reverse=reverse,
                                     d=d, tok_off=0, n_seq=b_ctx, t_seq=t_ctx, small_blk=small_blk)
            o_l, _ = _deltanet_dir(qkv, proj, state_delta[:, l, d], dn_a_log[l, d], dn_dt_bias[l, d],
                                   reverse=reverse, d=d, tok_off=geom.n_ctx, n_seq=b_lat, t_seq=t_lat,
                                   small_blk=small_blk)
            outs.append(jnp.concatenate([o_c, o_l], axis=0))
            states.append(s_c)
        new_states.append(jnp.stack(states, axis=1))
        s_b = _gmlp(proj, gm_norm[l], gm_w_s[l], gm_b_s[l], u_blk, vg_blk)
        merged = _merge(outs[0], outs[1], proj, s_b, dn_norm[l], w_branch_a[l], w_branch_b[l], z_blk, ga_blk, gb_blk)
        x = _out_projection(merged, w_out[l], x, mod, geom)
        wq_t = peer_w_query[l].reshape(D, -1).T.astype(BF16)
        keys = peer_sub_keys[l].reshape(PEER_HEADS * 2, PEER_NKEYS, -1).astype(BF16)
        h2_t, s0, s1, a, b, th = _peer_select(x, norm_ffn[l], mod, wq_t, keys, geom)
        x = _peer_dense(h2_t, peer_u[l].astype(BF16), peer_v[l].T.astype(BF16), s0, s1, a, b, th, x, mod, geom)

    y = _final_norm(x, norm_final)
    y_prompt = y[:geom.n_ctx].reshape(b_ctx, t_ctx, D)
    y_sample = y[geom.n_ctx:].reshape(b_lat, t_lat, D)
    new_state_delta = jnp.stack(new_states, axis=1).astype(x_prompt.dtype)
    return (y_prompt, y_sample, new_state_delta)
```

```python
import functools
import math

import numpy as np
import jax
import jax.numpy as jnp
from jax import lax
from jax.experimental import pallas as pl
from jax.experimental.pallas import tpu as pltpu

F32 = jnp.float32
BF16 = jnp.bfloat16

EPS = 1e-6
N_MOD = 6
GRID_W = 64
DN_HEADS = 8
DN_DK = 128
DN_CHUNK = 256
GM_GROUPS = 8
GM_CH = 128
GM_CHUNK = 128
PEER_HEADS = 8
PEER_NKEYS = 128
PEER_TOPK = 16
LANE = 128
SUBLANE = 8
VMEM_LIMIT = 56 * 1024 * 1024
NEG_BIG = -1e30

_HI = lax.Precision.HIGHEST


def _cparams(sem):
    return pltpu.CompilerParams(dimension_semantics=sem, vmem_limit_bytes=VMEM_LIMIT)


def _pick(n, cap):
    b = 1
    while b * 2 <= cap and n % (b * 2) == 0:
        b *= 2
    return b


def _dot(a, b):
    return jnp.dot(a.astype(BF16), b.astype(BF16), preferred_element_type=F32)


def _dot_nt(a, b):
    return lax.dot_general(a.astype(BF16), b.astype(BF16), (((1,), (1,)), ((), ())),
                           preferred_element_type=F32)


def _dot_tn(a, b):
    return lax.dot_general(a.astype(BF16), b.astype(BF16), (((0,), (0,)), ((), ())),
                           preferred_element_type=F32)


def _gelu(x):
    return 0.5 * x * (1.0 + jnp.tanh(math.sqrt(2.0 / math.pi) * (x + 0.044715 * (x * x * x))))


def _sigmoid(x):
    return 1.0 / (1.0 + jnp.exp(-x))


def _softplus(x):
    return jnp.maximum(x, 0.0) + jnp.log(1.0 + jnp.exp(-jnp.abs(x)))


def _mod_kernel(c_ref, w_ref, b_ref, o_ref):
    c = c_ref[...]
    o_ref[...] = _dot(c * _sigmoid(c), w_ref[...]) + b_ref[...]


def _modulation(cond, w_mod, b_mod):
    L, D, N = w_mod.shape
    R = cond.shape[0]
    tn = _pick(N, 1024)
    return pl.pallas_call(
        _mod_kernel,
        grid=(L, N // tn),
        in_specs=[pl.BlockSpec((R, D), lambda l, j: (0, 0)),
                  pl.BlockSpec((None, D, tn), lambda l, j: (l, 0, j)),
                  pl.BlockSpec((None, 1, tn), lambda l, j: (l, 0, j))],
        out_specs=pl.BlockSpec((None, R, tn), lambda l, j: (l, 0, j)),
        out_shape=jax.ShapeDtypeStruct((L, R, N), F32),
        compiler_params=_cparams(("parallel", "parallel")),
        name="adaln_mod",
    )(cond, w_mod, b_mod.reshape(L, 1, N))


class _Geom:
    def __init__(self, b_ctx, t_ctx, b_lat, t_lat):
        self.b_ctx, self.t_ctx, self.b_lat, self.t_lat = b_ctx, t_ctx, b_lat, t_lat
        self.n_ctx = b_ctx * t_ctx
        self.n_lat = b_lat * t_lat
        self.n_tok = self.n_ctx + self.n_lat

    def block(self, cap):
        return _pick(math.gcd(self.n_ctx, self.t_lat), cap)

    def mod_row(self, tb):
        n_ctx_blocks = self.n_ctx // tb
        per_seq = self.t_lat // tb

        def row(i):
            return jnp.where(i < n_ctx_blocks, 0, 1 + (i - n_ctx_blocks) // per_seq)

        return row


def _mod_block(geom, tb, which, D, ngrid):
    row = geom.mod_row(tb)
    if ngrid == 1:
        return pl.BlockSpec((None, None, 1, D), lambda i: (row(i), which, 0, 0))
    return pl.BlockSpec((None, None, 1, D), lambda i, j: (row(i), which, 0, 0))


def _inproj_kernel(x_ref, nw_ref, sc_ref, sh_ref, w_ref, o_ref, h_scr):
    @pl.when(pl.program_id(1) == 0)
    def _():
        x = x_ref[...]
        ms = jnp.mean(x * x, axis=-1, keepdims=True)
        y = x * lax.rsqrt(ms + EPS) * nw_ref[...]
        h_scr[...] = (y * (1.0 + sc_ref[...]) + sh_ref[...]).astype(BF16)

    o_ref[...] = jnp.dot(h_scr[...], w_ref[...], preferred_element_type=F32)


def _in_projection(x, norm_w, mod, w_bf16, geom):
    n_tok, D = x.shape
    N = w_bf16.shape[1]
    tb = geom.block(1024)
    tn = 1152 if N % 1152 == 0 else _pick(N, 1024)
    return pl.pallas_call(
        _inproj_kernel,
        grid=(n_tok // tb, N // tn),
        in_specs=[pl.BlockSpec((tb, D), lambda i, j: (i, 0)),
                  pl.BlockSpec((1, D), lambda i, j: (0, 0)),
                  _mod_block(geom, tb, 1, D, 2),
                  _mod_block(geom, tb, 0, D, 2),
                  pl.BlockSpec((D, tn), lambda i, j: (0, j))],
        out_specs=pl.BlockSpec((tb, tn), lambda i, j: (i, j)),
        out_shape=jax.ShapeDtypeStruct((n_tok, N), F32),
        scratch_shapes=[pltpu.VMEM((tb, D), BF16)],
        compiler_params=_cparams(("parallel", "arbitrary")),
        name="norm_in_proj",
    )(x, norm_w.reshape(1, D), mod, mod, w_bf16)


CONV_BLK = 256


def _conv_kernel(cur_ref, prev_ref, next_ref, w_ref, o_ref, *, geom):
    i = pl.program_id(0)
    j = pl.program_id(1)
    n_ctx_blocks = geom.n_ctx // CONV_BLK
    bps_ctx = geom.t_ctx // CONV_BLK
    bps_lat = geom.t_lat // CONV_BLK
    pos = jnp.where(i < n_ctx_blocks, i % bps_ctx, (i - n_ctx_blocks) % bps_lat)
    per = jnp.where(i < n_ctx_blocks, bps_ctx, bps_lat)
    is_start = pos == 0
    is_end = pos == per - 1

    x = cur_ref[...]
    prev_row = jnp.where(is_start, 0.0, prev_ref[SUBLANE - 1:SUBLANE, :])
    next_row = jnp.where(is_end, 0.0, next_ref[0:1, :])
    rid = lax.broadcasted_iota(jnp.int32, x.shape, 0)
    xm1 = jnp.where(rid == 0, prev_row, pltpu.roll(x, 1, axis=0))
    xp1 = jnp.where(rid == CONV_BLK - 1, next_row, pltpu.roll(x, CONV_BLK - 1, axis=0))
    w = w_ref[...]
    y = w[0:1, :] * xm1 + w[1:2, :] * x + w[2:3, :] * xp1
    y = y * _sigmoid(y)

    @pl.when(j == 2)
    def _():
        o_ref[...] = y.astype(BF16)

    @pl.when(j < 2)
    def _():
        scale = jnp.where(j == 0, DN_DK ** -0.5, 1.0).astype(F32)
        for h in range(DN_HEADS):
            seg = y[:, h * DN_DK:(h + 1) * DN_DK]
            ss = jnp.sum(seg * seg, axis=-1, keepdims=True)
            o_ref[:, h * DN_DK:(h + 1) * DN_DK] = (seg * (lax.rsqrt(ss + EPS) * scale)).astype(BF16)


def _conv_qkv(proj, conv_w, geom):
    n_tok = proj.shape[0]
    width = DN_HEADS * DN_DK
    nb8 = n_tok // SUBLANE
    r = CONV_BLK // SUBLANE
    return pl.pallas_call(
        functools.partial(_conv_kernel, geom=geom),
        grid=(n_tok // CONV_BLK, 3),
        in_specs=[pl.BlockSpec((CONV_BLK, width), lambda i, j: (i, j)),
                  pl.BlockSpec((SUBLANE, width), lambda i, j: (jnp.maximum(i * r - 1, 0), j)),
                  pl.BlockSpec((SUBLANE, width), lambda i, j: (jnp.minimum((i + 1) * r, nb8 - 1), j)),
                  pl.BlockSpec((3, width), lambda i, j: (0, j))],
        out_specs=pl.BlockSpec((CONV_BLK, width), lambda i, j: (i, j)),
        out_shape=jax.ShapeDtypeStruct((n_tok, 3 * width), BF16),
        compiler_params=_cparams(("parallel", "parallel")),
        name="conv_silu_l2",
    )(proj, proj, proj, conv_w)


DN_WIN = DN_CHUNK
_LEVELS = tuple(1 << i for i in range(DN_CHUNK.bit_length() - 1))


def _dn_constants(reverse):
    n = DN_CHUNK
    r = np.arange(n)[:, None]
    c = np.arange(n)[None, :]
    before = (c > r) if reverse else (c < r)
    incl = before | (r == c)
    lvl = []
    for s in _LEVELS:
        same = (r // (2 * s)) == (c // (2 * s))
        if reverse:
            m = same & ((r % (2 * s)) < s) & ((c % (2 * s)) >= s)
        else:
            m = same & ((r % (2 * s)) >= s) & ((c % (2 * s)) < s)
        lvl.append(m)
    masks = np.stack([before, incl] + lvl).astype(np.float32)
    cum_col = incl.astype(np.float32)
    return masks, cum_col, cum_col.T


def _dn_kernel(meta_ref, qf_ref, kf_ref, vf_ref, smf_ref, qb_ref, kb_ref, vb_ref, smb_ref, s0_ref,
               pa_ref, pdt_ref, msk_ref, cc_ref, cr_ref, of_ref, ob_ref, sfin_ref, s_scr, *, n_ctx_seq):
    j = pl.program_id(0)
    seq = meta_ref[2, j]
    is_ctx = seq < n_ctx_seq

    @pl.when(meta_ref[3, j] == 1)
    def _():
        s_scr[...] = jnp.where(is_ctx, 0.0, s0_ref[...])

    dirs = ((0, qf_ref, kf_ref, vf_ref, smf_ref, of_ref), (1, qb_ref, kb_ref, vb_ref, smb_ref, ob_ref))
    ch = []
    for d, q_ref, k_ref, v_ref, sm_ref, o_ref in dirs:
        sm = sm_ref[...]
        g_all = pa_ref[d] * _softplus(sm + pdt_ref[d])
        beta_all = _sigmoid(sm)
        gc_col = jnp.dot(cc_ref[d], g_all, precision=_HI, preferred_element_type=F32)
        gc_row = jnp.dot(g_all.T, cr_ref[d], precision=_HI, preferred_element_type=F32)
        before = msk_ref[d, 0]
        incl = msk_ref[d, 1]
        last = 0 if d == 1 else DN_CHUNK - 1
        for h in range(DN_HEADS):
            la = 16 + 8 * d + h
            lb = 8 * d + h
            cs = slice(h * DN_DK, (h + 1) * DN_DK)
            gcc = gc_col[:, la:la + 1]
            gcr = gc_row[la:la + 1, :]
            bcol = beta_all[:, lb:lb + 1]
            dec = jnp.exp(jnp.where(incl > 0, gcc - gcr, NEG_BIG))
            kh = k_ref[:, cs]
            kk = _dot_nt(kh, kh)
            m = (bcol * kk) * (dec * before)
            ch.append(dict(d=d, h=h, cs=cs, gcc=gcc, bcol=bcol, dec=dec, m=m, last=last, o_ref=o_ref,
                           q_ref=q_ref, k_ref=k_ref, v_ref=v_ref,
                           t=(incl - before) - m * msk_ref[d, 2]))
    for li in range(1, len(_LEVELS)):
        tmp = [_dot(c["t"], c["m"] * msk_ref[c["d"], 2 + li]) for c in ch]
        for c, x in zip(ch, tmp):
            c["t"] = c["t"] - _dot(x, c["t"])
    for c in ch:
        kh = c["k_ref"][:, c["cs"]]
        egc = jnp.exp(c["gcc"])
        rhs = jnp.concatenate([c["v_ref"][:, c["cs"]] * c["bcol"], kh * (c["bcol"] * egc)], axis=1)
        c["sol"] = _dot(c["t"], rhs)
        c["egc"] = egc
    for c in ch:
        d, h, cs, gcc = c["d"], c["h"], c["cs"], c["gcc"]
        qh = c["q_ref"][:, cs]
        kh = c["k_ref"][:, cs]
        u = c["sol"][:, :DN_DK]
        w = c["sol"][:, DN_DK:]
        s = s_scr[d, h]
        v_new = u - _dot(w, s)
        qk = _dot_nt(qh, kh)
        c["o_ref"][:, cs] = _dot(qh * c["egc"], s) + _dot(qk * c["dec"], v_new)
        gl = gcc[c["last"]:c["last"] + 1, :]
        s_scr[d, h] = s * jnp.exp(gl) + _dot_tn(kh * jnp.exp(gl - gcc), v_new)

    @pl.when((meta_ref[4, j] == 1) & is_ctx)
    def _():
        sfin_ref[...] = s_scr[...]


def _deltanet(qkv, proj, state0, a_log, dt_bias, geom, small_blk):
    n_tok = qkv.shape[0]
    width = DN_HEADS * DN_DK
    rows = []
    for b in range(geom.b_ctx + geom.b_lat):
        is_ctx = b < geom.b_ctx
        t_seq = geom.t_ctx if is_ctx else geom.t_lat
        base = (b * geom.t_ctx if is_ctx else geom.n_ctx + (b - geom.b_ctx) * geom.t_lat) // DN_WIN
        n_win = t_seq // DN_WIN
        for w in range(n_win):
            rows.append((base + w, base + n_win - 1 - w, b, int(w == 0), int(w == n_win - 1)))
    meta = jnp.asarray(np.array(rows, np.int32).T)
    n_steps = len(rows)

    consts = [_dn_constants(False), _dn_constants(True)]
    masks = jnp.asarray(np.stack([c[0] for c in consts]))
    cum_col = jnp.asarray(np.stack([c[1] for c in consts]))
    cum_row = jnp.asarray(np.stack([c[2] for c in consts]))
    lanes = jnp.arange(LANE)
    pa, pdt = [], []
    for d in range(2):
        hd = jnp.clip(lanes - (16 + 8 * d), 0, DN_HEADS - 1)
        valid = (lanes >= 16 + 8 * d) & (lanes < 24 + 8 * d)
        pa.append(jnp.where(valid, -jnp.exp(a_log[d].astype(F32))[hd], 0.0))
        pdt.append(jnp.where(valid, dt_bias[d].astype(F32)[hd], 0.0))
    pa = jnp.stack(pa).reshape(2, 1, LANE)
    pdt = jnp.stack(pdt).reshape(2, 1, LANE)

    one = pl.Buffered(1)
    n_ctx_seq = geom.b_ctx
    st_shape = (2, DN_HEADS, DN_DK, DN_DK)

    def win(row, col):
        return pl.BlockSpec((DN_WIN, width if col < 3 else LANE),
                            lambda j, meta: (meta[row, j], col if col < 3 else small_blk))

    def const(a):
        nd = a.ndim
        return pl.BlockSpec(a.shape, lambda j, meta: (0,) * nd, pipeline_mode=one)

    grid_spec = pltpu.PrefetchScalarGridSpec(
        num_scalar_prefetch=1,
        grid=(n_steps,),
        in_specs=[win(0, 0), win(0, 1), win(0, 2), win(0, 3), win(1, 0), win(1, 1), win(1, 2), win(1, 3),
                  pl.BlockSpec((None,) + st_shape, lambda j, meta: (jnp.maximum(meta[2, j] - n_ctx_seq, 0), 0, 0, 0, 0)),
                  const(pa), const(pdt), const(masks), const(cum_col), const(cum_row)],
        out_specs=[pl.BlockSpec((DN_WIN, width), lambda j, meta: (meta[0, j], 0)),
                   pl.BlockSpec((DN_WIN, width), lambda j, meta: (meta[1, j], 0)),
                   pl.BlockSpec((None,) + st_shape, lambda j, meta: (jnp.minimum(meta[2, j], n_ctx_seq - 1), 0, 0, 0, 0))],
        scratch_shapes=[pltpu.VMEM(st_shape, F32)],
    )
    return pl.pallas_call(
        functools.partial(_dn_kernel, n_ctx_seq=n_ctx_seq),
        grid_spec=grid_spec,
        out_shape=[jax.ShapeDtypeStruct((n_tok, width), F32), jax.ShapeDtypeStruct((n_tok, width), F32),
                   jax.ShapeDtypeStruct((n_ctx_seq,) + st_shape, F32)],
        compiler_params=_cparams(("arbitrary",)),
        name="deltanet",
    )(meta, qkv, qkv, qkv, proj, qkv, qkv, qkv, proj, state0, pa, pdt, masks, cum_col, cum_row)


def _gmlp_kernel(u_ref, vg_ref, nw_ref, ws_ref, bias_ref, o_ref, *, tb):
    vg = _gelu(vg_ref[...])
    ms = jnp.mean(vg * vg, axis=-1, keepdims=True)
    vn = (vg * lax.rsqrt(ms + EPS) * nw_ref[...]).astype(BF16)
    for ch in range(tb // GM_CHUNK):
        rs = slice(ch * GM_CHUNK, (ch + 1) * GM_CHUNK)
        for g in range(GM_GROUPS):
            cs = slice(g * GM_CH, (g + 1) * GM_CH)
            mixed = jnp.dot(ws_ref[g], vn[rs, cs], preferred_element_type=F32) + bias_ref[g]
            o_ref[rs, cs] = (_gelu(u_ref[rs, cs]) * mixed).astype(BF16)


def _gmlp(proj, gm_norm, w_s, b_s, u_blk, vg_blk):
    n_tok = proj.shape[0]
    width = GM_GROUPS * GM_CH
    tb = 256
    bias = jnp.broadcast_to(b_s.astype(F32)[:, :, None], (GM_GROUPS, GM_CHUNK, GM_CH))
    return pl.pallas_call(
        functools.partial(_gmlp_kernel, tb=tb),
        grid=(n_tok // tb,),
        in_specs=[pl.BlockSpec((tb, width), lambda i: (i, u_blk)),
                  pl.BlockSpec((tb, width), lambda i: (i, vg_blk)),
                  pl.BlockSpec((1, width), lambda i: (0, 0)),
                  pl.BlockSpec((GM_GROUPS, GM_CHUNK, GM_CHUNK), lambda i: (0, 0, 0)),
                  pl.BlockSpec((GM_GROUPS, GM_CHUNK, GM_CH), lambda i: (0, 0, 0))],
        out_specs=pl.BlockSpec((tb, width), lambda i: (i, 0)),
        out_shape=jax.ShapeDtypeStruct((n_tok, width), BF16),
        compiler_params=_cparams(("parallel",)),
        name="gmlp_gate",
    )(proj, proj, gm_norm.reshape(1, width), w_s.astype(BF16), bias)


def _merge_kernel(of_ref, ob_ref, z_ref, sb_ref, ga_ref, gb_ref, dnw_ref, wa_ref, wb_ref, o_ref, on_scr):
    for h in range(DN_HEADS):
        cs = slice(h * DN_DK, (h + 1) * DN_DK)
        o = of_ref[:, cs] + ob_ref[:, cs]
        ms = jnp.mean(o * o, axis=-1, keepdims=True)
        z = z_ref[:, cs]
        on_scr[:, cs] = (o * lax.rsqrt(ms + EPS) * dnw_ref[...] * (z * _sigmoid(z))).astype(BF16)
    p_a = jnp.dot(on_scr[...], wa_ref[...], preferred_element_type=F32)
    p_b = jnp.dot(sb_ref[...], wb_ref[...], preferred_element_type=F32)
    o_ref[...] = (_sigmoid(ga_ref[...]) * p_a + _sigmoid(gb_ref[...]) * p_b).astype(BF16)


def _merge(o_f, o_b, proj, s_b, dn_norm, w_a, w_b, z_blk, ga_blk, gb_blk):
    n_tok = proj.shape[0]
    wa_in, D = w_a.shape
    wb_in = w_b.shape[0]
    tb = 256
    return pl.pallas_call(
        _merge_kernel,
        grid=(n_tok // tb,),
        in_specs=[pl.BlockSpec((tb, wa_in), lambda i: (i, 0)),
                  pl.BlockSpec((tb, wa_in), lambda i: (i, 0)),
                  pl.BlockSpec((tb, wa_in), lambda i: (i, z_blk)),
                  pl.BlockSpec((tb, wb_in), lambda i: (i, 0)),
                  pl.BlockSpec((tb, D), lambda i: (i, ga_blk)),
                  pl.BlockSpec((tb, D), lambda i: (i, gb_blk)),
                  pl.BlockSpec((1, DN_DK), lambda i: (0, 0)),
                  pl.BlockSpec((wa_in, D), lambda i: (0, 0)),
                  pl.BlockSpec((wb_in, D), lambda i: (0, 0))],
        out_specs=pl.BlockSpec((tb, D), lambda i: (i, 0)),
        out_shape=jax.ShapeDtypeStruct((n_tok, D), BF16),
        scratch_shapes=[pltpu.VMEM((tb, wa_in), BF16)],
        compiler_params=_cparams(("parallel",)),
        name="branch_merge",
    )(o_f, o_b, proj, s_b, proj, proj, dn_norm.reshape(1, DN_DK), w_a.astype(BF16), w_b.astype(BF16))


def _outproj_kernel(m_ref, w_ref, x_ref, gt_ref, o_ref):
    y = jnp.dot(m_ref[...], w_ref[...], preferred_element_type=F32)
    o_ref[...] = x_ref[...] + gt_ref[...] * y


def _out_projection(merged, w_out, x, mod, geom):
    n_tok, D = x.shape
    tb = geom.block(512)
    return pl.pallas_call(
        _outproj_kernel,
        grid=(n_tok // tb,),
        in_specs=[pl.BlockSpec((tb, D), lambda i: (i, 0)),
                  pl.BlockSpec((D, D), lambda i: (0, 0)),
                  pl.BlockSpec((tb, D), lambda i: (i, 0)),
                  _mod_block(geom, tb, 2, D, 1)],
        out_specs=pl.BlockSpec((tb, D), lambda i: (i, 0)),
        out_shape=jax.ShapeDtypeStruct((n_tok, D), F32),
        compiler_params=_cparams(("parallel",)),
        name="out_proj_residual",
    )(merged, w_out.astype(BF16), x, mod)


def _top_rows(cur, k):
    vals = []
    for _ in range(k):
        m = jnp.max(cur, axis=0, keepdims=True)
        vals.append(m)
        cur = jnp.where(cur == m, -jnp.inf, cur)
    return vals


_PAIRS = [(a, b) for a in range(PEER_TOPK) for b in range(PEER_TOPK) if (a + 1) * (b + 1) <= PEER_TOPK + 1]


def _peer_select_kernel(x_ref, nw_ref, sc_ref, sh_ref, wq_ref, keys_ref,
                        ht_ref, thr_ref, s1_ref, a_ref, b_ref):
    x = x_ref[...]
    ms = jnp.mean(x * x, axis=-1, keepdims=True)
    y = x * lax.rsqrt(ms + EPS) * nw_ref[...]
    h2_t = (y * (1.0 + sc_ref[...]) + sh_ref[...]).T.astype(BF16)
    ht_ref[...] = h2_t
    q_t = jnp.dot(wq_ref[...], h2_t, preferred_element_type=F32)
    dk = PEER_NKEYS
    for h in range(PEER_HEADS):
        svs = []
        scores = []
        for p in range(2):
            r0 = (h * 2 + p) * dk
            s = jnp.dot(keys_ref[h * 2 + p], q_t[r0:r0 + dk, :].astype(BF16), preferred_element_type=F32)
            svs.append(_top_rows(s, PEER_TOPK))
            scores.append(s)
        cand = jnp.concatenate([svs[0][a] + svs[1][b] for a, b in _PAIRS], axis=0)
        best = _top_rows(cand, PEER_TOPK + 1)
        m0 = svs[0][0]
        m1 = svs[1][0]
        mx = best[0]
        z = jnp.zeros_like(mx)
        for v in best[:PEER_TOPK]:
            z = z + jnp.exp(v - mx)
        theta = 0.5 * (best[PEER_TOPK - 1] + best[PEER_TOPK])
        thr_ref[h] = theta - scores[0]
        s1_ref[h] = scores[1]
        a_ref[h] = jnp.exp(scores[0] - m0)
        b_ref[h] = jnp.exp(scores[1] - m1) * (0.5 * jnp.exp(m0 + m1 - mx) / z)


def _peer_select(x, norm_w, mod, wq_t, keys, geom):
    n_tok, D = x.shape
    tb = geom.block(256)
    H, NK = PEER_HEADS, PEER_NKEYS
    big = jax.ShapeDtypeStruct((H, NK, n_tok), F32)
    big_spec = pl.BlockSpec((H, NK, tb), lambda i: (0, 0, i))
    return pl.pallas_call(
        _peer_select_kernel,
        grid=(n_tok // tb,),
        in_specs=[pl.BlockSpec((tb, D), lambda i: (i, 0)),
                  pl.BlockSpec((1, D), lambda i: (0, 0)),
                  _mod_block(geom, tb, 4, D, 1),
                  _mod_block(geom, tb, 3, D, 1),
                  pl.BlockSpec(wq_t.shape, lambda i: (0, 0)),
                  pl.BlockSpec(keys.shape, lambda i: (0, 0, 0))],
        out_specs=[pl.BlockSpec((D, tb), lambda i: (0, i)), big_spec, big_spec, big_spec, big_spec],
        out_shape=[jax.ShapeDtypeStruct((D, n_tok), BF16), big, big, big, big],
        compiler_params=_cparams(("parallel",)),
        name="peer_select",
    )(x, norm_w.reshape(1, D), mod, mod, wq_t, keys)


PEER_EBLK = 1024
PEER_TBLK = 512
_PD_ROWS = 64
_PD_IGRP = 4
_GELU_C1 = math.sqrt(2.0 / math.pi)
_GELU_C2 = _GELU_C1 * 0.044715


def _peer_dense_kernel(ht_ref, u_ref, vt_ref, thr_ref, a_ref, s1_ref, b_ref, x_ref, gt_ref, *rest, n_e, tb, n_ctx_blk):
    if n_ctx_blk is None:
        o_ref, acc_scr, act0, act1, w0, w1 = rest
    else:
        fw_ref, oc_ref, ol_ref, acc_scr, act0, act1, w0, w1 = rest
    s = pl.program_id(0)
    e_c = jnp.maximum(s - 2, 0) % n_e

    @pl.when(s == 0)
    def _():
        for r in (act0, act1, w0, w1):
            r[...] = jnp.zeros_like(r)

    @pl.when(e_c == 0)
    def _():
        acc_scr[...] = jnp.zeros_like(acc_scr)

    def stages(act_w, act_r, w_w, w_r):
        act_w[...] = jnp.dot(u_ref[...], ht_ref[...], preferred_element_type=F32)
        acc_scr[...] += jnp.dot(vt_ref[...], w_r[...], preferred_element_type=F32)
        nk = PEER_NKEYS
        for tc in range(tb // LANE):
            cs = slice(tc * LANE, (tc + 1) * LANE)
            for jh in range(nk // _PD_ROWS):
                js = slice(jh * _PD_ROWS, (jh + 1) * _PD_ROWS)
                for ig in range(PEER_EBLK // nk // _PD_IGRP):
                    g = [jnp.zeros((_PD_ROWS, LANE), F32) for _ in range(_PD_IGRP)]
                    for h in range(PEER_HEADS):
                        s1t = s1_ref[h, js, cs]
                        bt = b_ref[h, js, cs]
                        for k in range(_PD_IGRP):
                            ii = ig * _PD_IGRP + k
                            sel = s1t >= thr_ref[h, ii:ii + 1, cs]
                            g[k] = g[k] + jnp.where(sel, bt * a_ref[h, ii:ii + 1, cs], 0.0)
                    for k in range(_PD_IGRP):
                        r0 = (ig * _PD_IGRP + k) * nk + jh * _PD_ROWS
                        xa = act_r[r0:r0 + _PD_ROWS, cs]
                        tt = jnp.tanh(xa * (_GELU_C1 + _GELU_C2 * (xa * xa)))
                        w_w[r0:r0 + _PD_ROWS, cs] = ((xa + xa * tt) * g[k]).astype(BF16)

    @pl.when(s % 2 == 0)
    def _():
        stages(act0, act1, w1, w0)

    @pl.when(s % 2 == 1)
    def _():
        stages(act1, act0, w0, w1)

    done = (e_c == n_e - 1) & (s >= 2)
    if n_ctx_blk is None:
        @pl.when(done)
        def _():
            o_ref[...] = x_ref[...] + gt_ref[...] * acc_scr[...].T
    else:
        t_c = jnp.maximum(s - 2, 0) // n_e

        def normed():
            xn = x_ref[...] + gt_ref[...] * acc_scr[...].T
            ms = jnp.mean(xn * xn, axis=-1, keepdims=True)
            return xn * lax.rsqrt(ms + EPS) * fw_ref[...]

        @pl.when(done & (t_c < n_ctx_blk))
        def _():
            oc_ref[...] = normed()

        @pl.when(done & (t_c >= n_ctx_blk))
        def _():
            ol_ref[...] = normed()


def _peer_dense(h2_t, u_bf16, vt_bf16, thr, s1, a, b_half, x, mod, geom, final_w=None):
    n_tok, D = x.shape
    n_exp = u_bf16.shape[0]
    tb = geom.block(PEER_TBLK)
    n_e = n_exp // PEER_EBLK
    n_steps = (n_tok // tb) * n_e
    H, NK = PEER_HEADS, PEER_NKEYS
    ib = PEER_EBLK // NK
    row = geom.mod_row(tb)
    one = pl.Buffered(1)
    n_ctx_blk = geom.n_ctx // tb

    def step(s, lag):
        return jnp.clip(s - lag, 0, n_steps - 1)

    def tok(s, lag):
        return step(s, lag) // n_e

    def exp(s, lag):
        return step(s, lag) % n_e

    in_specs = [pl.BlockSpec((D, tb), lambda s: (0, tok(s, 0)), pipeline_mode=one),
                pl.BlockSpec((PEER_EBLK, D), lambda s: (exp(s, 0), 0)),
                pl.BlockSpec((D, PEER_EBLK), lambda s: (0, exp(s, 2))),
                pl.BlockSpec((H, ib, tb), lambda s: (0, exp(s, 1), tok(s, 1))),
                pl.BlockSpec((H, ib, tb), lambda s: (0, exp(s, 1), tok(s, 1))),
                pl.BlockSpec((H, NK, tb), lambda s: (0, 0, tok(s, 1)), pipeline_mode=one),
                pl.BlockSpec((H, NK, tb), lambda s: (0, 0, tok(s, 1)), pipeline_mode=one),
                pl.BlockSpec((tb, D), lambda s: (tok(s, 2), 0), pipeline_mode=one),
                pl.BlockSpec((None, None, 1, D), lambda s: (row(tok(s, 2)), 5, 0, 0))]
    args = [h2_t, u_bf16, vt_bf16, thr, a, s1, b_half, x, mod]
    if final_w is None:
        out_specs = pl.BlockSpec((tb, D), lambda s: (tok(s, 2), 0), pipeline_mode=one)
        out_shape = jax.ShapeDtypeStruct((n_tok, D), F32)
    else:
        in_specs.append(pl.BlockSpec((1, D), lambda s: (0, 0)))
        args.append(final_w.reshape(1, D))
        out_specs = [pl.BlockSpec((tb, D), lambda s: (jnp.minimum(tok(s, 2), n_ctx_blk - 1), 0), pipeline_mode=one),
                     pl.BlockSpec((tb, D), lambda s: (jnp.maximum(tok(s, 2) - n_ctx_blk, 0), 0), pipeline_mode=one)]
        out_shape = [jax.ShapeDtypeStruct((geom.n_ctx, D), F32), jax.ShapeDtypeStruct((geom.n_lat, D), F32)]
    return pl.pallas_call(
        functools.partial(_peer_dense_kernel, n_e=n_e, tb=tb, n_ctx_blk=None if final_w is None else n_ctx_blk),
        grid=(n_steps + 2,),
        in_specs=in_specs,
        out_specs=out_specs,
        out_shape=out_shape,
        scratch_shapes=[pltpu.VMEM((D, tb), F32),
                        pltpu.VMEM((PEER_EBLK, tb), F32), pltpu.VMEM((PEER_EBLK, tb), F32),
                        pltpu.VMEM((PEER_EBLK, tb), BF16), pltpu.VMEM((PEER_EBLK, tb), BF16)],
        compiler_params=_cparams(("arbitrary",)),
        name="peer_dense",
    )(*args)


def _grid_pos_embed(T, D):
    rows = T // GRID_W
    r, col = jnp.meshgrid(jnp.arange(rows, dtype=F32), jnp.arange(GRID_W, dtype=F32), indexing='ij')
    quarter = D // 4
    freq = 1.0 / (10000.0 ** (jnp.arange(quarter, dtype=F32) / quarter))

    def enc(p):
        ang = p.reshape(-1, 1) * freq
        return jnp.concatenate([jnp.sin(ang), jnp.cos(ang)], axis=-1)

    return jnp.concatenate([enc(r), enc(col)], axis=-1)


def _permute_w_in(w_in_l, qk_w, v_w, gm_w, D):
    n_small = 4 * DN_HEADS
    o_small = 2 * qk_w + 2 * v_w
    pad = LANE - n_small
    return jnp.concatenate([w_in_l[:, :o_small], w_in_l[:, o_small + n_small:],
                            w_in_l[:, o_small:o_small + n_small],
                            jnp.zeros((D, pad), w_in_l.dtype)], axis=1).astype(BF16)


def kernel(x_prompt, x_sample, state_delta, c, c_ctx, w_mod, b_mod, norm_mix, w_in, conv_w, dn_a_log,
           dn_dt_bias, dn_norm, gm_norm, gm_w_s, gm_b_s, w_branch_a, w_branch_b, w_out, norm_ffn,
           peer_w_query, peer_sub_keys, peer_u, peer_v, norm_final):
    b_ctx, t_ctx, D = x_prompt.shape
    b_lat, t_lat, _ = x_sample.shape
    depth = w_mod.shape[0]
    geom = _Geom(b_ctx, t_ctx, b_lat, t_lat)
    qk_w = DN_HEADS * DN_DK
    v_w = qk_w
    gm_w = GM_GROUPS * GM_CH
    assert t_ctx % CONV_BLK == 0 and t_lat % CONV_BLK == 0 and qk_w == gm_w and D == 2 * qk_w

    xs = x_sample + _grid_pos_embed(t_lat, D).astype(x_sample.dtype)[None]
    x = jnp.concatenate([x_prompt.reshape(geom.n_ctx, D), xs.reshape(geom.n_lat, D)], axis=0)

    n_rows = -(-(1 + b_lat) // SUBLANE) * SUBLANE
    cond = jnp.concatenate([c_ctx[None, :], c, jnp.zeros((n_rows - 1 - b_lat, D), c.dtype)], axis=0)
    mod_all = _modulation(cond, w_mod, b_mod).reshape(depth, n_rows, N_MOD, 1, D)

    z_blk, u_blk, vg_blk = 3, 4, 5
    ga_blk, gb_blk = 3, 4
    small_blk = (3 * qk_w + v_w + 2 * gm_w + 2 * D) // LANE

    new_states = []
    for l in range(depth):
        mod = mod_all[l]
        w_in_l = _permute_w_in(w_in[l], qk_w, v_w, gm_w, D)
        proj = _in_projection(x, norm_mix[l], mod, w_in_l, geom)
        qkv = _conv_qkv(proj, conv_w[l], geom)
        o_f, o_b, s_ctx = _deltanet(qkv, proj, state_delta[:, l], dn_a_log[l], dn_dt_bias[l], geom, small_blk)
        new_states.append(s_ctx)
        s_b = _gmlp(proj, gm_norm[l], gm_w_s[l], gm_b_s[l], u_blk, vg_blk)
        merged = _merge(o_f, o_b, proj, s_b, dn_norm[l], w_branch_a[l], w_branch_b[l], z_blk, ga_blk, gb_blk)
        x = _out_projection(merged, w_out[l], x, mod, geom)
        wq_t = peer_w_query[l].reshape(D, -1).T.astype(BF16)
        keys = peer_sub_keys[l].reshape(PEER_HEADS * 2, PEER_NKEYS, -1).astype(BF16)
        h2_t, thr, s1, a, b = _peer_select(x, norm_ffn[l], mod, wq_t, keys, geom)
        x = _peer_dense(h2_t, peer_u[l].astype(BF16), peer_v[l].T.astype(BF16), thr, s1, a, b, x, mod, geom,
                        final_w=norm_final if l == depth - 1 else None)

    y_prompt = x[0].reshape(b_ctx, t_ctx, D)
    y_sample = x[1].reshape(b_lat, t_lat, D)
    new_state_delta = jnp.stack(new_states, axis=1).astype(x_prompt.dtype)
    return (y_prompt, y_sample, new_state_delta)
```

```python
import functools
import math

import numpy as np
import jax
import jax.numpy as jnp
from jax import lax
from jax.experimental import pallas as pl
from jax.experimental.pallas import tpu as pltpu

F32 = jnp.float32
BF16 = jnp.bfloat16

EPS = 1e-6
N_MOD = 6
GRID_W = 64
DN_HEADS = 8
DN_DK = 128
DN_CHUNK = 128
GM_GROUPS = 8
GM_CH = 128
GM_CHUNK = 128
PEER_HEADS = 8
PEER_NKEYS = 128
PEER_TOPK = 16
LANE = 128
SUBLANE = 8
VMEM_LIMIT = 56 * 1024 * 1024
NEG_BIG = -1e30

_HI = lax.Precision.HIGHEST


def _cparams(sem):
    return pltpu.CompilerParams(dimension_semantics=sem, vmem_limit_bytes=VMEM_LIMIT)


def _pick(n, cap):
    b = 1
    while b * 2 <= cap and n % (b * 2) == 0:
        b *= 2
    return b


def _dot(a, b):
    return jnp.dot(a.astype(BF16), b.astype(BF16), preferred_element_type=F32)


def _dot_nt(a, b):
    return lax.dot_general(a.astype(BF16), b.astype(BF16), (((1,), (1,)), ((), ())),
                           preferred_element_type=F32)


def _dot_tn(a, b):
    return lax.dot_general(a.astype(BF16), b.astype(BF16), (((0,), (0,)), ((), ())),
                           preferred_element_type=F32)


def _gelu(x):
    return 0.5 * x * (1.0 + jnp.tanh(math.sqrt(2.0 / math.pi) * (x + 0.044715 * (x * x * x))))


def _sigmoid(x):
    return 1.0 / (1.0 + jnp.exp(-x))


def _softplus(x):
    return jnp.maximum(x, 0.0) + jnp.log(1.0 + jnp.exp(-jnp.abs(x)))


def _mod_kernel(c_ref, w_ref, b_ref, o_ref):
    c = c_ref[...]
    o_ref[...] = _dot(c * _sigmoid(c), w_ref[...]) + b_ref[...]


def _modulation(cond, w_mod, b_mod):
    L, D, N = w_mod.shape
    R = cond.shape[0]
    tn = _pick(N, 1024)
    return pl.pallas_call(
        _mod_kernel,
        grid=(L, N // tn),
        in_specs=[pl.BlockSpec((R, D), lambda l, j: (0, 0)),
                  pl.BlockSpec((None, D, tn), lambda l, j: (l, 0, j)),
                  pl.BlockSpec((None, 1, tn), lambda l, j: (l, 0, j))],
        out_specs=pl.BlockSpec((None, R, tn), lambda l, j: (l, 0, j)),
        out_shape=jax.ShapeDtypeStruct((L, R, N), F32),
        compiler_params=_cparams(("parallel", "parallel")),
        name="adaln_mod",
    )(cond, w_mod, b_mod.reshape(L, 1, N))


class _Geom:
    def __init__(self, b_ctx, t_ctx, b_lat, t_lat):
        self.b_ctx, self.t_ctx, self.b_lat, self.t_lat = b_ctx, t_ctx, b_lat, t_lat
        self.n_ctx = b_ctx * t_ctx
        self.n_lat = b_lat * t_lat
        self.n_tok = self.n_ctx + self.n_lat

    def block(self, cap):
        return _pick(math.gcd(self.n_ctx, self.t_lat), cap)

    def mod_row(self, tb):
        n_ctx_blocks = self.n_ctx // tb
        per_seq = self.t_lat // tb

        def row(i):
            return jnp.where(i < n_ctx_blocks, 0, 1 + (i - n_ctx_blocks) // per_seq)

        return row


def _mod_block(geom, tb, which, D, ngrid):
    row = geom.mod_row(tb)
    if ngrid == 1:
        return pl.BlockSpec((None, None, 1, D), lambda i: (row(i), which, 0, 0))
    return pl.BlockSpec((None, None, 1, D), lambda i, j: (row(i), which, 0, 0))


def _inproj_kernel(x_ref, nw_ref, sc_ref, sh_ref, w_ref, o_ref, h_scr):
    @pl.when(pl.program_id(1) == 0)
    def _():
        x = x_ref[...]
        ms = jnp.mean(x * x, axis=-1, keepdims=True)
        y = x * lax.rsqrt(ms + EPS) * nw_ref[...]
        h_scr[...] = (y * (1.0 + sc_ref[...]) + sh_ref[...]).astype(BF16)

    o_ref[...] = jnp.dot(h_scr[...], w_ref[...], preferred_element_type=F32)


def _in_projection(x, norm_w, mod, w_bf16, geom):
    n_tok, D = x.shape
    N = w_bf16.shape[1]
    tb = geom.block(1024)
    tn = 1152 if N % 1152 == 0 else _pick(N, 1024)
    return pl.pallas_call(
        _inproj_kernel,
        grid=(n_tok // tb, N // tn),
        in_specs=[pl.BlockSpec((tb, D), lambda i, j: (i, 0)),
                  pl.BlockSpec((1, D), lambda i, j: (0, 0)),
                  _mod_block(geom, tb, 1, D, 2),
                  _mod_block(geom, tb, 0, D, 2),
                  pl.BlockSpec((D, tn), lambda i, j: (0, j))],
        out_specs=pl.BlockSpec((tb, tn), lambda i, j: (i, j)),
        out_shape=jax.ShapeDtypeStruct((n_tok, N), F32),
        scratch_shapes=[pltpu.VMEM((tb, D), BF16)],
        compiler_params=_cparams(("parallel", "arbitrary")),
        name="norm_in_proj",
    )(x, norm_w.reshape(1, D), mod, mod, w_bf16)


CONV_BLK = 256


def _conv_kernel(cur_ref, prev_ref, next_ref, w_ref, o_ref, *, geom):
    i = pl.program_id(0)
    j = pl.program_id(1)
    n_ctx_blocks = geom.n_ctx // CONV_BLK
    bps_ctx = geom.t_ctx // CONV_BLK
    bps_lat = geom.t_lat // CONV_BLK
    pos = jnp.where(i < n_ctx_blocks, i % bps_ctx, (i - n_ctx_blocks) % bps_lat)
    per = jnp.where(i < n_ctx_blocks, bps_ctx, bps_lat)
    is_start = pos == 0
    is_end = pos == per - 1

    x = cur_ref[...]
    prev_row = jnp.where(is_start, 0.0, prev_ref[SUBLANE - 1:SUBLANE, :])
    next_row = jnp.where(is_end, 0.0, next_ref[0:1, :])
    rid = lax.broadcasted_iota(jnp.int32, x.shape, 0)
    xm1 = jnp.where(rid == 0, prev_row, pltpu.roll(x, 1, axis=0))
    xp1 = jnp.where(rid == CONV_BLK - 1, next_row, pltpu.roll(x, CONV_BLK - 1, axis=0))
    w = w_ref[...]
    y = w[0:1, :] * xm1 + w[1:2, :] * x + w[2:3, :] * xp1
    y = y * _sigmoid(y)

    @pl.when(j == 2)
    def _():
        o_ref[...] = y.astype(BF16)

    @pl.when(j < 2)
    def _():
        scale = jnp.where(j == 0, DN_DK ** -0.5, 1.0).astype(F32)
        for h in range(DN_HEADS):
            seg = y[:, h * DN_DK:(h + 1) * DN_DK]
            ss = jnp.sum(seg * seg, axis=-1, keepdims=True)
            o_ref[:, h * DN_DK:(h + 1) * DN_DK] = (seg * (lax.rsqrt(ss + EPS) * scale)).astype(BF16)


def _conv_qkv(proj, conv_w, geom):
    n_tok = proj.shape[0]
    width = DN_HEADS * DN_DK
    nb8 = n_tok // SUBLANE
    r = CONV_BLK // SUBLANE
    return pl.pallas_call(
        functools.partial(_conv_kernel, geom=geom),
        grid=(n_tok // CONV_BLK, 3),
        in_specs=[pl.BlockSpec((CONV_BLK, width), lambda i, j: (i, j)),
                  pl.BlockSpec((SUBLANE, width), lambda i, j: (jnp.maximum(i * r - 1, 0), j)),
                  pl.BlockSpec((SUBLANE, width), lambda i, j: (jnp.minimum((i + 1) * r, nb8 - 1), j)),
                  pl.BlockSpec((3, width), lambda i, j: (0, j))],
        out_specs=pl.BlockSpec((CONV_BLK, width), lambda i, j: (i, j)),
        out_shape=jax.ShapeDtypeStruct((n_tok, 3 * width), BF16),
        compiler_params=_cparams(("parallel", "parallel")),
        name="conv_silu_l2",
    )(proj, proj, proj, conv_w)


DN_WIN = 256
_LEVELS = tuple(1 << i for i in range(DN_CHUNK.bit_length() - 1))


def _dn_constants(reverse):
    n = DN_CHUNK
    r = np.arange(n)[:, None]
    c = np.arange(n)[None, :]
    before = (c > r) if reverse else (c < r)
    incl = before | (r == c)
    lvl = []
    for s in _LEVELS:
        same = (r // (2 * s)) == (c // (2 * s))
        if reverse:
            m = same & ((r % (2 * s)) < s) & ((c % (2 * s)) >= s)
        else:
            m = same & ((r % (2 * s)) >= s) & ((c % (2 * s)) < s)
        lvl.append(m)
    masks = np.stack([before, incl] + lvl).astype(np.float32)
    cum_col = incl.astype(np.float32)
    return masks, cum_col, cum_col.T


def _dn_kernel(meta_ref, qf_ref, kf_ref, vf_ref, smf_ref, qb_ref, kb_ref, vb_ref, smb_ref, s0_ref,
               pa_ref, pdt_ref, msk_ref, cc_ref, cr_ref, of_ref, ob_ref, sfin_ref, s_scr, *, n_ctx_seq):
    j = pl.program_id(0)
    seq = meta_ref[2, j]
    is_ctx = seq < n_ctx_seq

    @pl.when(meta_ref[3, j] == 1)
    def _():
        s_scr[...] = jnp.where(is_ctx, 0.0, s0_ref[...])

    n_chunks = DN_WIN // DN_CHUNK
    dirs = ((0, qf_ref, kf_ref, vf_ref, smf_ref, of_ref), (1, qb_ref, kb_ref, vb_ref, smb_ref, ob_ref))
    ch = {}
    for d, q_ref, k_ref, v_ref, sm_ref, o_ref in dirs:
        sm = sm_ref[...]
        g_all = pa_ref[d] * _softplus(sm + pdt_ref[d])
        beta_all = _sigmoid(sm)
        g_t = g_all.T
        before = msk_ref[d, 0]
        incl = msk_ref[d, 1]
        for c in range(n_chunks):
            rs = slice(c * DN_CHUNK, (c + 1) * DN_CHUNK)
            gc_col = jnp.dot(cc_ref[d], g_all[rs, :], precision=_HI, preferred_element_type=F32)
            gc_row = jnp.dot(g_t[:, rs], cr_ref[d], precision=_HI, preferred_element_type=F32)
            for h in range(DN_HEADS):
                la = 16 + 8 * d + h
                lb = 8 * d + h
                cs = slice(h * DN_DK, (h + 1) * DN_DK)
                gcc = gc_col[:, la:la + 1]
                gcr = gc_row[la:la + 1, :]
                bcol = beta_all[rs, lb:lb + 1]
                dec = jnp.exp(jnp.where(incl > 0, gcc - gcr, NEG_BIG))
                kh = k_ref[rs, cs]
                m = (bcol * _dot_nt(kh, kh)) * (dec * before)
                ch[d, c, h] = dict(d=d, rs=rs, cs=cs, gcc=gcc, bcol=bcol, dec=dec, m=m, o_ref=o_ref,
                                   q_ref=q_ref, k_ref=k_ref, v_ref=v_ref,
                                   t=(incl - before) - m * msk_ref[d, 2])
    chains = list(ch.values())
    for li in range(1, len(_LEVELS)):
        tmp = [_dot(c["t"], c["m"] * msk_ref[c["d"], 2 + li]) for c in chains]
        for c, x in zip(chains, tmp):
            c["t"] = c["t"] - _dot(x, c["t"])
    for c in chains:
        kh = c["k_ref"][c["rs"], c["cs"]]
        egc = jnp.exp(c["gcc"])
        rhs = jnp.concatenate([c["v_ref"][c["rs"], c["cs"]] * c["bcol"], kh * (c["bcol"] * egc)], axis=1)
        c["sol"] = _dot(c["t"], rhs)
        c["egc"] = egc
        c["qk"] = _dot_nt(c["q_ref"][c["rs"], c["cs"]], kh)
    for step in range(n_chunks):
        for d in range(2):
            cidx = step if d == 0 else n_chunks - 1 - step
            last = DN_CHUNK - 1 if d == 0 else 0
            for h in range(DN_HEADS):
                c = ch[d, cidx, h]
                rs, cs, gcc = c["rs"], c["cs"], c["gcc"]
                qh = c["q_ref"][rs, cs]
                kh = c["k_ref"][rs, cs]
                u = c["sol"][:, :DN_DK]
                w = c["sol"][:, DN_DK:]
                s = s_scr[d, h]
                v_new = u - _dot(w, s)
                c["o_ref"][rs, cs] = _dot(qh * c["egc"], s) + _dot(c["qk"] * c["dec"], v_new)
                gl = gcc[last:last + 1, :]
                s_scr[d, h] = s * jnp.exp(gl) + _dot_tn(kh * jnp.exp(gl - gcc), v_new)

    @pl.when((meta_ref[4, j] == 1) & is_ctx)
    def _():
        sfin_ref[...] = s_scr[...]


def _deltanet(qkv, proj, state0, a_log, dt_bias, geom, small_blk):
    n_tok = qkv.shape[0]
    width = DN_HEADS * DN_DK
    rows = []
    for b in range(geom.b_ctx + geom.b_lat):
        is_ctx = b < geom.b_ctx
        t_seq = geom.t_ctx if is_ctx else geom.t_lat
        base = (b * geom.t_ctx if is_ctx else geom.n_ctx + (b - geom.b_ctx) * geom.t_lat) // DN_WIN
        n_win = t_seq // DN_WIN
        for w in range(n_win):
            rows.append((base + w, base + n_win - 1 - w, b, int(w == 0), int(w == n_win - 1)))
    meta = jnp.asarray(np.array(rows, np.int32).T)
    n_steps = len(rows)

    consts = [_dn_constants(False), _dn_constants(True)]
    masks = jnp.asarray(np.stack([c[0] for c in consts]))
    cum_col = jnp.asarray(np.stack([c[1] for c in consts]))
    cum_row = jnp.asarray(np.stack([c[2] for c in consts]))
    lanes = jnp.arange(LANE)
    pa, pdt = [], []
    for d in range(2):
        hd = jnp.clip(lanes - (16 + 8 * d), 0, DN_HEADS - 1)
        valid = (lanes >= 16 + 8 * d) & (lanes < 24 + 8 * d)
        pa.append(jnp.where(valid, -jnp.exp(a_log[d].astype(F32))[hd], 0.0))
        pdt.append(jnp.where(valid, dt_bias[d].astype(F32)[hd], 0.0))
    pa = jnp.stack(pa).reshape(2, 1, LANE)
    pdt = jnp.stack(pdt).reshape(2, 1, LANE)

    one = pl.Buffered(1)
    n_ctx_seq = geom.b_ctx
    st_shape = (2, DN_HEADS, DN_DK, DN_DK)

    def win(row, col):
        return pl.BlockSpec((DN_WIN, width if col < 3 else LANE),
                            lambda j, meta: (meta[row, j], col if col < 3 else small_blk))

    def const(a):
        nd = a.ndim
        return pl.BlockSpec(a.shape, lambda j, meta: (0,) * nd, pipeline_mode=one)

    grid_spec = pltpu.PrefetchScalarGridSpec(
        num_scalar_prefetch=1,
        grid=(n_steps,),
        in_specs=[win(0, 0), win(0, 1), win(0, 2), win(0, 3), win(1, 0), win(1, 1), win(1, 2), win(1, 3),
                  pl.BlockSpec((None,) + st_shape, lambda j, meta: (jnp.maximum(meta[2, j] - n_ctx_seq, 0), 0, 0, 0, 0)),
                  const(pa), const(pdt), const(masks), const(cum_col), const(cum_row)],
        out_specs=[pl.BlockSpec((DN_WIN, width), lambda j, meta: (meta[0, j], 0)),
                   pl.BlockSpec((DN_WIN, width), lambda j, meta: (meta[1, j], 0)),
                   pl.BlockSpec((None,) + st_shape, lambda j, meta: (jnp.minimum(meta[2, j], n_ctx_seq - 1), 0, 0, 0, 0))],
        scratch_shapes=[pltpu.VMEM(st_shape, F32)],
    )
    return pl.pallas_call(
        functools.partial(_dn_kernel, n_ctx_seq=n_ctx_seq),
        grid_spec=grid_spec,
        out_shape=[jax.ShapeDtypeStruct((n_tok, width), F32), jax.ShapeDtypeStruct((n_tok, width), F32),
                   jax.ShapeDtypeStruct((n_ctx_seq,) + st_shape, F32)],
        compiler_params=_cparams(("arbitrary",)),
        name="deltanet",
    )(meta, qkv, qkv, qkv, proj, qkv, qkv, qkv, proj, state0, pa, pdt, masks, cum_col, cum_row)


def _gmlp_kernel(u_ref, vg_ref, nw_ref, ws_ref, bias_ref, o_ref, *, tb):
    vg = _gelu(vg_ref[...])
    ms = jnp.mean(vg * vg, axis=-1, keepdims=True)
    vn = (vg * lax.rsqrt(ms + EPS) * nw_ref[...]).astype(BF16)
    for ch in range(tb // GM_CHUNK):
        rs = slice(ch * GM_CHUNK, (ch + 1) * GM_CHUNK)
        for g in range(GM_GROUPS):
            cs = slice(g * GM_CH, (g + 1) * GM_CH)
            mixed = jnp.dot(ws_ref[g], vn[rs, cs], preferred_element_type=F32) + bias_ref[g]
            o_ref[rs, cs] = (_gelu(u_ref[rs, cs]) * mixed).astype(BF16)


def _gmlp(proj, gm_norm, w_s, b_s, u_blk, vg_blk):
    n_tok = proj.shape[0]
    width = GM_GROUPS * GM_CH
    tb = 256
    bias = jnp.broadcast_to(b_s.astype(F32)[:, :, None], (GM_GROUPS, GM_CHUNK, GM_CH))
    return pl.pallas_call(
        functools.partial(_gmlp_kernel, tb=tb),
        grid=(n_tok // tb,),
        in_specs=[pl.BlockSpec((tb, width), lambda i: (i, u_blk)),
                  pl.BlockSpec((tb, width), lambda i: (i, vg_blk)),
                  pl.BlockSpec((1, width), lambda i: (0, 0)),
                  pl.BlockSpec((GM_GROUPS, GM_CHUNK, GM_CHUNK), lambda i: (0, 0, 0)),
                  pl.BlockSpec((GM_GROUPS, GM_CHUNK, GM_CH), lambda i: (0, 0, 0))],
        out_specs=pl.BlockSpec((tb, width), lambda i: (i, 0)),
        out_shape=jax.ShapeDtypeStruct((n_tok, width), BF16),
        compiler_params=_cparams(("parallel",)),
        name="gmlp_gate",
    )(proj, proj, gm_norm.reshape(1, width), w_s.astype(BF16), bias)


def _merge_kernel(of_ref, ob_ref, z_ref, sb_ref, ga_ref, gb_ref, dnw_ref, wa_ref, wb_ref, o_ref, on_scr):
    for h in range(DN_HEADS):
        cs = slice(h * DN_DK, (h + 1) * DN_DK)
        o = of_ref[:, cs] + ob_ref[:, cs]
        ms = jnp.mean(o * o, axis=-1, keepdims=True)
        z = z_ref[:, cs]
        on_scr[:, cs] = (o * lax.rsqrt(ms + EPS) * dnw_ref[...] * (z * _sigmoid(z))).astype(BF16)
    p_a = jnp.dot(on_scr[...], wa_ref[...], preferred_element_type=F32)
    p_b = jnp.dot(sb_ref[...], wb_ref[...], preferred_element_type=F32)
    o_ref[...] = (_sigmoid(ga_ref[...]) * p_a + _sigmoid(gb_ref[...]) * p_b).astype(BF16)


def _merge(o_f, o_b, proj, s_b, dn_norm, w_a, w_b, z_blk, ga_blk, gb_blk):
    n_tok = proj.shape[0]
    wa_in, D = w_a.shape
    wb_in = w_b.shape[0]
    tb = 256
    return pl.pallas_call(
        _merge_kernel,
        grid=(n_tok // tb,),
        in_specs=[pl.BlockSpec((tb, wa_in), lambda i: (i, 0)),
                  pl.BlockSpec((tb, wa_in), lambda i: (i, 0)),
                  pl.BlockSpec((tb, wa_in), lambda i: (i, z_blk)),
                  pl.BlockSpec((tb, wb_in), lambda i: (i, 0)),
                  pl.BlockSpec((tb, D), lambda i: (i, ga_blk)),
                  pl.BlockSpec((tb, D), lambda i: (i, gb_blk)),
                  pl.BlockSpec((1, DN_DK), lambda i: (0, 0)),
                  pl.BlockSpec((wa_in, D), lambda i: (0, 0)),
                  pl.BlockSpec((wb_in, D), lambda i: (0, 0))],
        out_specs=pl.BlockSpec((tb, D), lambda i: (i, 0)),
        out_shape=jax.ShapeDtypeStruct((n_tok, D), BF16),
        scratch_shapes=[pltpu.VMEM((tb, wa_in), BF16)],
        compiler_params=_cparams(("parallel",)),
        name="branch_merge",
    )(o_f, o_b, proj, s_b, proj, proj, dn_norm.reshape(1, DN_DK), w_a.astype(BF16), w_b.astype(BF16))


def _outproj_kernel(m_ref, w_ref, x_ref, gt_ref, o_ref):
    y = jnp.dot(m_ref[...], w_ref[...], preferred_element_type=F32)
    o_ref[...] = x_ref[...] + gt_ref[...] * y


def _out_projection(merged, w_out, x, mod, geom):
    n_tok, D = x.shape
    tb = geom.block(512)
    return pl.pallas_call(
        _outproj_kernel,
        grid=(n_tok // tb,),
        in_specs=[pl.BlockSpec((tb, D), lambda i: (i, 0)),
                  pl.BlockSpec((D, D), lambda i: (0, 0)),
                  pl.BlockSpec((tb, D), lambda i: (i, 0)),
                  _mod_block(geom, tb, 2, D, 1)],
        out_specs=pl.BlockSpec((tb, D), lambda i: (i, 0)),
        out_shape=jax.ShapeDtypeStruct((n_tok, D), F32),
        compiler_params=_cparams(("parallel",)),
        name="out_proj_residual",
    )(merged, w_out.astype(BF16), x, mod)


def _top_rows(cur, k):
    vals = []
    for _ in range(k):
        m = jnp.max(cur, axis=0, keepdims=True)
        vals.append(m)
        cur = jnp.where(cur == m, -jnp.inf, cur)
    return vals


_PAIRS = [(a, b) for a in range(PEER_TOPK) for b in range(PEER_TOPK) if (a + 1) * (b + 1) <= PEER_TOPK + 1]


def _peer_select_kernel(x_ref, nw_ref, sc_ref, sh_ref, wq_ref, keys_ref,
                        ht_ref, thr_ref, s1_ref, a_ref, b_ref):
    x = x_ref[...]
    ms = jnp.mean(x * x, axis=-1, keepdims=True)
    y = x * lax.rsqrt(ms + EPS) * nw_ref[...]
    h2_t = (y * (1.0 + sc_ref[...]) + sh_ref[...]).T.astype(BF16)
    ht_ref[...] = h2_t
    q_t = jnp.dot(wq_ref[...], h2_t, preferred_element_type=F32)
    dk = PEER_NKEYS
    for h in range(PEER_HEADS):
        svs = []
        scores = []
        for p in range(2):
            r0 = (h * 2 + p) * dk
            s = jnp.dot(keys_ref[h * 2 + p], q_t[r0:r0 + dk, :].astype(BF16), preferred_element_type=F32)
            svs.append(_top_rows(s, PEER_TOPK))
            scores.append(s)
        cand = jnp.concatenate([svs[0][a] + svs[1][b] for a, b in _PAIRS], axis=0)
        best = _top_rows(cand, PEER_TOPK + 1)
        m0 = svs[0][0]
        m1 = svs[1][0]
        mx = best[0]
        z = jnp.zeros_like(mx)
        for v in best[:PEER_TOPK]:
            z = z + jnp.exp(v - mx)
        theta = 0.5 * (best[PEER_TOPK - 1] + best[PEER_TOPK])
        thr_ref[h] = theta - scores[0]
        s1_ref[h] = scores[1]
        a_ref[h] = jnp.exp(scores[0] - m0)
        b_ref[h] = jnp.exp(scores[1] - m1) * (0.5 * jnp.exp(m0 + m1 - mx) / z)


def _peer_select(x, norm_w, mod, wq_t, keys, geom):
    n_tok, D = x.shape
    tb = geom.block(256)
    H, NK = PEER_HEADS, PEER_NKEYS
    big = jax.ShapeDtypeStruct((H, NK, n_tok), F32)
    big_spec = pl.BlockSpec((H, NK, tb), lambda i: (0, 0, i))
    return pl.pallas_call(
        _peer_select_kernel,
        grid=(n_tok // tb,),
        in_specs=[pl.BlockSpec((tb, D), lambda i: (i, 0)),
                  pl.BlockSpec((1, D), lambda i: (0, 0)),
                  _mod_block(geom, tb, 4, D, 1),
                  _mod_block(geom, tb, 3, D, 1),
                  pl.BlockSpec(wq_t.shape, lambda i: (0, 0)),
                  pl.BlockSpec(keys.shape, lambda i: (0, 0, 0))],
        out_specs=[pl.BlockSpec((D, tb), lambda i: (0, i)), big_spec, big_spec, big_spec, big_spec],
        out_shape=[jax.ShapeDtypeStruct((D, n_tok), BF16), big, big, big, big],
        compiler_params=_cparams(("parallel",)),
        name="peer_select",
    )(x, norm_w.reshape(1, D), mod, mod, wq_t, keys)


PEER_EBLK = 1024
PEER_TBLK = 512
_PD_ROWS = 64
_PD_IGRP = 4
_GELU_C1 = math.sqrt(2.0 / math.pi)
_GELU_C2 = _GELU_C1 * 0.044715


def _peer_dense_kernel(ht_ref, u_ref, vt_ref, thr_ref, a_ref, s1_ref, b_ref, x_ref, gt_ref, *rest, n_e, tb, n_ctx_blk):
    if n_ctx_blk is None:
        o_ref, acc_scr, act0, act1, w0, w1 = rest
    else:
        fw_ref, oc_ref, ol_ref, acc_scr, act0, act1, w0, w1 = rest
    s = pl.program_id(0)
    e_c = jnp.maximum(s - 2, 0) % n_e

    @pl.when(s == 0)
    def _():
        for r in (act0, act1, w0, w1):
            r[...] = jnp.zeros_like(r)

    @pl.when(e_c == 0)
    def _():
        acc_scr[...] = jnp.zeros_like(acc_scr)

    def stages(act_w, act_r, w_w, w_r):
        act_w[...] = jnp.dot(u_ref[...], ht_ref[...], preferred_element_type=F32)
        acc_scr[...] += jnp.dot(vt_ref[...], w_r[...], preferred_element_type=F32)
        nk = PEER_NKEYS
        for tc in range(tb // LANE):
            cs = slice(tc * LANE, (tc + 1) * LANE)
            for jh in range(nk // _PD_ROWS):
                js = slice(jh * _PD_ROWS, (jh + 1) * _PD_ROWS)
                for ig in range(PEER_EBLK // nk // _PD_IGRP):
                    g = [jnp.zeros((_PD_ROWS, LANE), F32) for _ in range(_PD_IGRP)]
                    for h in range(PEER_HEADS):
                        s1t = s1_ref[h, js, cs]
                        bt = b_ref[h, js, cs]
                        for k in range(_PD_IGRP):
                            ii = ig * _PD_IGRP + k
                            sel = s1t >= thr_ref[h, ii:ii + 1, cs]
                            g[k] = g[k] + jnp.where(sel, bt * a_ref[h, ii:ii + 1, cs], 0.0)
                    for k in range(_PD_IGRP):
                        r0 = (ig * _PD_IGRP + k) * nk + jh * _PD_ROWS
                        xa = act_r[r0:r0 + _PD_ROWS, cs]
                        tt = jnp.tanh(xa * (_GELU_C1 + _GELU_C2 * (xa * xa)))
                        w_w[r0:r0 + _PD_ROWS, cs] = ((xa + xa * tt) * g[k]).astype(BF16)

    @pl.when(s % 2 == 0)
    def _():
        stages(act0, act1, w1, w0)

    @pl.when(s % 2 == 1)
    def _():
        stages(act1, act0, w0, w1)

    done = (e_c == n_e - 1) & (s >= 2)
    if n_ctx_blk is None:
        @pl.when(done)
        def _():
            o_ref[...] = x_ref[...] + gt_ref[...] * acc_scr[...].T
    else:
        t_c = jnp.maximum(s - 2, 0) // n_e

        def normed():
            xn = x_ref[...] + gt_ref[...] * acc_scr[...].T
            ms = jnp.mean(xn * xn, axis=-1, keepdims=True)
            return xn * lax.rsqrt(ms + EPS) * fw_ref[...]

        @pl.when(done & (t_c < n_ctx_blk))
        def _():
            oc_ref[...] = normed()

        @pl.when(done & (t_c >= n_ctx_blk))
        def _():
            ol_ref[...] = normed()


def _peer_dense(h2_t, u_bf16, vt_bf16, thr, s1, a, b_half, x, mod, geom, final_w=None):
    n_tok, D = x.shape
    n_exp = u_bf16.shape[0]
    tb = geom.block(PEER_TBLK)
    n_e = n_exp // PEER_EBLK
    n_steps = (n_tok // tb) * n_e
    H, NK = PEER_HEADS, PEER_NKEYS
    ib = PEER_EBLK // NK
    row = geom.mod_row(tb)
    one = pl.Buffered(1)
    n_ctx_blk = geom.n_ctx // tb

    def step(s, lag):
        return jnp.clip(s - lag, 0, n_steps - 1)

    def tok(s, lag):
        return step(s, lag) // n_e

    def exp(s, lag):
        return step(s, lag) % n_e

    in_specs = [pl.BlockSpec((D, tb), lambda s: (0, tok(s, 0))),
                pl.BlockSpec((PEER_EBLK, D), lambda s: (exp(s, 0), 0)),
                pl.BlockSpec((D, PEER_EBLK), lambda s: (0, exp(s, 2))),
                pl.BlockSpec((H, ib, tb), lambda s: (0, exp(s, 1), tok(s, 1))),
                pl.BlockSpec((H, ib, tb), lambda s: (0, exp(s, 1), tok(s, 1))),
                pl.BlockSpec((H, NK, tb), lambda s: (0, 0, tok(s, 1)), pipeline_mode=one),
                pl.BlockSpec((H, NK, tb), lambda s: (0, 0, tok(s, 1)), pipeline_mode=one),
                pl.BlockSpec((tb, D), lambda s: (tok(s, 2), 0), pipeline_mode=one),
                pl.BlockSpec((None, None, 1, D), lambda s: (row(tok(s, 2)), 5, 0, 0))]
    args = [h2_t, u_bf16, vt_bf16, thr, a, s1, b_half, x, mod]
    if final_w is None:
        out_specs = pl.BlockSpec((tb, D), lambda s: (tok(s, 2), 0))
        out_shape = jax.ShapeDtypeStruct((n_tok, D), F32)
    else:
        in_specs.append(pl.BlockSpec((1, D), lambda s: (0, 0)))
        args.append(final_w.reshape(1, D))
        out_specs = [pl.BlockSpec((tb, D), lambda s: (jnp.minimum(tok(s, 2), n_ctx_blk - 1), 0), pipeline_mode=one),
                     pl.BlockSpec((tb, D), lambda s: (jnp.maximum(tok(s, 2) - n_ctx_blk, 0), 0), pipeline_mode=one)]
        out_shape = [jax.ShapeDtypeStruct((geom.n_ctx, D), F32), jax.ShapeDtypeStruct((geom.n_lat, D), F32)]
    return pl.pallas_call(
        functools.partial(_peer_dense_kernel, n_e=n_e, tb=tb, n_ctx_blk=None if final_w is None else n_ctx_blk),
        grid=(n_steps + 2,),
        in_specs=in_specs,
        out_specs=out_specs,
        out_shape=out_shape,
        scratch_shapes=[pltpu.VMEM((D, tb), F32),
                        pltpu.VMEM((PEER_EBLK, tb), F32), pltpu.VMEM((PEER_EBLK, tb), F32),
                        pltpu.VMEM((PEER_EBLK, tb), BF16), pltpu.VMEM((PEER_EBLK, tb), BF16)],
        compiler_params=_cparams(("arbitrary",)),
        name="peer_dense",
    )(*args)


def _grid_pos_embed(T, D):
    rows = T // GRID_W
    r, col = jnp.meshgrid(jnp.arange(rows, dtype=F32), jnp.arange(GRID_W, dtype=F32), indexing='ij')
    quarter = D // 4
    freq = 1.0 / (10000.0 ** (jnp.arange(quarter, dtype=F32) / quarter))

    def enc(p):
        ang = p.reshape(-1, 1) * freq
        return jnp.concatenate([jnp.sin(ang), jnp.cos(ang)], axis=-1)

    return jnp.concatenate([enc(r), enc(col)], axis=-1)


def _permute_w_in(w_in_l, qk_w, v_w, gm_w, D):
    n_small = 4 * DN_HEADS
    o_small = 2 * qk_w + 2 * v_w
    pad = LANE - n_small
    return jnp.concatenate([w_in_l[:, :o_small], w_in_l[:, o_small + n_small:],
                            w_in_l[:, o_small:o_small + n_small],
                            jnp.zeros((D, pad), w_in_l.dtype)], axis=1).astype(BF16)


def kernel(x_prompt, x_sample, state_delta, c, c_ctx, w_mod, b_mod, norm_mix, w_in, conv_w, dn_a_log,
           dn_dt_bias, dn_norm, gm_norm, gm_w_s, gm_b_s, w_branch_a, w_branch_b, w_out, norm_ffn,
           peer_w_query, peer_sub_keys, peer_u, peer_v, norm_final):
    b_ctx, t_ctx, D = x_prompt.shape
    b_lat, t_lat, _ = x_sample.shape
    depth = w_mod.shape[0]
    geom = _Geom(b_ctx, t_ctx, b_lat, t_lat)
    qk_w = DN_HEADS * DN_DK
    v_w = qk_w
    gm_w = GM_GROUPS * GM_CH
    assert t_ctx % CONV_BLK == 0 and t_lat % CONV_BLK == 0 and qk_w == gm_w and D == 2 * qk_w

    xs = x_sample + _grid_pos_embed(t_lat, D).astype(x_sample.dtype)[None]
    x = jnp.concatenate([x_prompt.reshape(geom.n_ctx, D), xs.reshape(geom.n_lat, D)], axis=0)

    n_rows = -(-(1 + b_lat) // SUBLANE) * SUBLANE
    cond = jnp.concatenate([c_ctx[None, :], c, jnp.zeros((n_rows - 1 - b_lat, D), c.dtype)], axis=0)
    mod_all = _modulation(cond, w_mod, b_mod).reshape(depth, n_rows, N_MOD, 1, D)

    z_blk, u_blk, vg_blk = 3, 4, 5
    ga_blk, gb_blk = 3, 4
    small_blk = (3 * qk_w + v_w + 2 * gm_w + 2 * D) // LANE

    new_states = []
    for l in range(depth):
        mod = mod_all[l]
        w_in_l = _permute_w_in(w_in[l], qk_w, v_w, gm_w, D)
        proj = _in_projection(x, norm_mix[l], mod, w_in_l, geom)
        qkv = _conv_qkv(proj, conv_w[l], geom)
        o_f, o_b, s_ctx = _deltanet(qkv, proj, state_delta[:, l], dn_a_log[l], dn_dt_bias[l], geom, small_blk)
        new_states.append(s_ctx)
        s_b = _gmlp(proj, gm_norm[l], gm_w_s[l], gm_b_s[l], u_blk, vg_blk)
        merged = _merge(o_f, o_b, proj, s_b, dn_norm[l], w_branch_a[l], w_branch_b[l], z_blk, ga_blk, gb_blk)
        x = _out_projection(merged, w_out[l], x, mod, geom)
        wq_t = peer_w_query[l].reshape(D, -1).T.astype(BF16)
        keys = peer_sub_keys[l].reshape(PEER_HEADS * 2, PEER_NKEYS, -1).astype(BF16)
        h2_t, thr, s1, a, b = _peer_select(x, norm_ffn[l], mod, wq_t, keys, geom)
        x = _peer_dense(h2_t, peer_u[l].astype(BF16), peer_v[l].T.astype(BF16), thr, s1, a, b, x, mod, geom,
                        final_w=norm_final if l == depth - 1 else None)

    y_prompt = x[0].reshape(b_ctx, t_ctx, D)
    y_sample = x[1].reshape(b_lat, t_lat, D)
    new_state_delta = jnp.stack(new_states, axis=1).astype(x_prompt.dtype)
    return (y_prompt, y_sample, new_state_delta)
```

```python
import functools
import math

import numpy as np
import jax
import jax.numpy as jnp
from jax import lax
from jax.experimental import pallas as pl
from jax.experimental.pallas import tpu as pltpu

F32 = jnp.float32
BF16 = jnp.bfloat16

EPS = 1e-6
N_MOD = 6
GRID_W = 64
DN_HEADS = 8
DN_DK = 128
DN_CHUNK = 128
GM_GROUPS = 8
GM_CH = 128
GM_CHUNK = 128
PEER_HEADS = 8
PEER_NKEYS = 128
PEER_TOPK = 16
LANE = 128
SUBLANE = 8
VMEM_LIMIT = 56 * 1024 * 1024
NEG_BIG = -1e30

_HI = lax.Precision.HIGHEST


def _cparams(sem):
    return pltpu.CompilerParams(dimension_semantics=sem, vmem_limit_bytes=VMEM_LIMIT)


def _pick(n, cap):
    b = 1
    while b * 2 <= cap and n % (b * 2) == 0:
        b *= 2
    return b


def _dot(a, b):
    return jnp.dot(a.astype(BF16), b.astype(BF16), preferred_element_type=F32)


def _dot_nt(a, b):
    return lax.dot_general(a.astype(BF16), b.astype(BF16), (((1,), (1,)), ((), ())),
                           preferred_element_type=F32)


def _dot_tn(a, b):
    return lax.dot_general(a.astype(BF16), b.astype(BF16), (((0,), (0,)), ((), ())),
                           preferred_element_type=F32)


def _gelu(x):
    return 0.5 * x * (1.0 + jnp.tanh(math.sqrt(2.0 / math.pi) * (x + 0.044715 * (x * x * x))))


def _sigmoid(x):
    return 1.0 / (1.0 + jnp.exp(-x))


def _softplus(x):
    return jnp.maximum(x, 0.0) + jnp.log(1.0 + jnp.exp(-jnp.abs(x)))


def _mod_kernel(c_ref, w_ref, b_ref, o_ref):
    c = c_ref[...]
    o_ref[...] = _dot(c * _sigmoid(c), w_ref[...]) + b_ref[...]


def _modulation(cond, w_mod, b_mod):
    L, D, N = w_mod.shape
    R = cond.shape[0]
    tn = _pick(N, 1024)
    return pl.pallas_call(
        _mod_kernel,
        grid=(L, N // tn),
        in_specs=[pl.BlockSpec((R, D), lambda l, j: (0, 0)),
                  pl.BlockSpec((None, D, tn), lambda l, j: (l, 0, j)),
                  pl.BlockSpec((None, 1, tn), lambda l, j: (l, 0, j))],
        out_specs=pl.BlockSpec((None, R, tn), lambda l, j: (l, 0, j)),
        out_shape=jax.ShapeDtypeStruct((L, R, N), F32),
        compiler_params=_cparams(("parallel", "parallel")),
        name="adaln_mod",
    )(cond, w_mod, b_mod.reshape(L, 1, N))


class _Geom:
    def __init__(self, b_ctx, t_ctx, b_lat, t_lat):
        self.b_ctx, self.t_ctx, self.b_lat, self.t_lat = b_ctx, t_ctx, b_lat, t_lat
        self.n_ctx = b_ctx * t_ctx
        self.n_lat = b_lat * t_lat
        self.n_tok = self.n_ctx + self.n_lat

    def block(self, cap):
        return _pick(math.gcd(self.n_ctx, self.t_lat), cap)

    def mod_row(self, tb):
        n_ctx_blocks = self.n_ctx // tb
        per_seq = self.t_lat // tb

        def row(i):
            return jnp.where(i < n_ctx_blocks, 0, 1 + (i - n_ctx_blocks) // per_seq)

        return row


def _mod_block(geom, tb, which, D, ngrid):
    row = geom.mod_row(tb)
    if ngrid == 1:
        return pl.BlockSpec((None, None, 1, D), lambda i: (row(i), which, 0, 0))
    return pl.BlockSpec((None, None, 1, D), lambda i, j: (row(i), which, 0, 0))


PROJ_BLK = 1024
N_GATES = 4 * DN_HEADS


def _inproj_kernel(x_ref, nw_ref, sc_ref, sh_ref, wa_ref, wb_ref, ws_ref, o_ref, sm_ref, h_scr, *, n_a):
    j = pl.program_id(1)

    @pl.when(j == 0)
    def _():
        x = x_ref[...]
        ms = jnp.mean(x * x, axis=-1, keepdims=True)
        y = x * lax.rsqrt(ms + EPS) * nw_ref[...]
        h_scr[...] = (y * (1.0 + sc_ref[...]) + sh_ref[...]).astype(BF16)
        sm_ref[...] = jnp.dot(h_scr[...], ws_ref[...], preferred_element_type=F32)

    @pl.when(j < n_a)
    def _():
        o_ref[...] = jnp.dot(h_scr[...], wa_ref[...], preferred_element_type=F32)

    @pl.when(j >= n_a)
    def _():
        o_ref[...] = jnp.dot(h_scr[...], wb_ref[...], preferred_element_type=F32)


def _in_projection(x, norm_w, mod, w_in_l, geom):
    n_tok, D = x.shape
    w = w_in_l.astype(BF16)
    o_small = 4 * PROJ_BLK
    n_a = o_small // PROJ_BLK
    w_b = w[:, o_small + N_GATES:]
    n_b = w_b.shape[1] // PROJ_BLK
    w_s = jnp.pad(w[:, o_small:o_small + N_GATES], ((0, 0), (0, LANE - N_GATES)))
    tb = geom.block(1024)
    return pl.pallas_call(
        functools.partial(_inproj_kernel, n_a=n_a),
        grid=(n_tok // tb, n_a + n_b),
        in_specs=[pl.BlockSpec((tb, D), lambda i, j: (i, 0)),
                  pl.BlockSpec((1, D), lambda i, j: (0, 0)),
                  _mod_block(geom, tb, 1, D, 2),
                  _mod_block(geom, tb, 0, D, 2),
                  pl.BlockSpec((D, PROJ_BLK), lambda i, j: (0, jnp.minimum(j, n_a - 1))),
                  pl.BlockSpec((D, PROJ_BLK), lambda i, j: (0, jnp.maximum(j - n_a, 0))),
                  pl.BlockSpec((D, LANE), lambda i, j: (0, 0))],
        out_specs=[pl.BlockSpec((tb, PROJ_BLK), lambda i, j: (i, j)),
                   pl.BlockSpec((tb, LANE), lambda i, j: (i, 0))],
        out_shape=[jax.ShapeDtypeStruct((n_tok, (n_a + n_b) * PROJ_BLK), F32),
                   jax.ShapeDtypeStruct((n_tok, LANE), F32)],
        scratch_shapes=[pltpu.VMEM((tb, D), BF16)],
        compiler_params=_cparams(("parallel", "arbitrary")),
        name="norm_in_proj",
    )(x, norm_w.reshape(1, D), mod, mod, w, w_b, w_s)


CONV_BLK = 256


CONV_ROWS = 64


def _conv_kernel(cur_ref, prev_ref, next_ref, w_ref, o_ref, *, geom):
    i = pl.program_id(0)
    j = pl.program_id(1)
    n_ctx_blocks = geom.n_ctx // CONV_BLK
    bps_ctx = geom.t_ctx // CONV_BLK
    bps_lat = geom.t_lat // CONV_BLK
    pos = jnp.where(i < n_ctx_blocks, i % bps_ctx, (i - n_ctx_blocks) % bps_lat)
    per = jnp.where(i < n_ctx_blocks, bps_ctx, bps_lat)
    before_blk = jnp.where(pos == 0, 0.0, prev_ref[SUBLANE - 1:SUBLANE, :])
    after_blk = jnp.where(pos == per - 1, 0.0, next_ref[0:1, :])
    w = w_ref[...]
    n_chunks = CONV_BLK // CONV_ROWS
    rid = lax.broadcasted_iota(jnp.int32, (CONV_ROWS, w.shape[1]), 0)

    def conv_silu(c):
        r0 = c * CONV_ROWS
        x = cur_ref[r0:r0 + CONV_ROWS, :]
        row_before = before_blk if c == 0 else cur_ref[r0 - 1:r0, :]
        row_after = after_blk if c == n_chunks - 1 else cur_ref[r0 + CONV_ROWS:r0 + CONV_ROWS + 1, :]
        xm1 = jnp.where(rid == 0, row_before, pltpu.roll(x, 1, axis=0))
        xp1 = jnp.where(rid == CONV_ROWS - 1, row_after, pltpu.roll(x, CONV_ROWS - 1, axis=0))
        y = w[0:1, :] * xm1 + w[1:2, :] * x + w[2:3, :] * xp1
        return y * _sigmoid(y)

    @pl.when(j == 2)
    def _():
        for c in range(n_chunks):
            o_ref[c * CONV_ROWS:(c + 1) * CONV_ROWS, :] = conv_silu(c).astype(BF16)

    @pl.when(j < 2)
    def _():
        scale = jnp.where(j == 0, DN_DK ** -0.5, 1.0).astype(F32)
        for c in range(n_chunks):
            y = conv_silu(c)
            for h in range(DN_HEADS):
                seg = y[:, h * DN_DK:(h + 1) * DN_DK]
                ss = jnp.sum(seg * seg, axis=-1, keepdims=True)
                o_ref[c * CONV_ROWS:(c + 1) * CONV_ROWS, h * DN_DK:(h + 1) * DN_DK] = (
                    seg * (lax.rsqrt(ss + EPS) * scale)).astype(BF16)


def _conv_qkv(proj, conv_w, geom):
    n_tok = proj.shape[0]
    width = DN_HEADS * DN_DK
    nb8 = n_tok // SUBLANE
    r = CONV_BLK // SUBLANE
    return pl.pallas_call(
        functools.partial(_conv_kernel, geom=geom),
        grid=(n_tok // CONV_BLK, 3),
        in_specs=[pl.BlockSpec((CONV_BLK, width), lambda i, j: (i, j)),
                  pl.BlockSpec((SUBLANE, width), lambda i, j: (jnp.maximum(i * r - 1, 0), j)),
                  pl.BlockSpec((SUBLANE, width), lambda i, j: (jnp.minimum((i + 1) * r, nb8 - 1), j)),
                  pl.BlockSpec((3, width), lambda i, j: (0, j))],
        out_specs=pl.BlockSpec((CONV_BLK, width), lambda i, j: (i, j)),
        out_shape=jax.ShapeDtypeStruct((n_tok, 3 * width), BF16),
        compiler_params=_cparams(("parallel", "parallel")),
        name="conv_silu_l2",
    )(proj, proj, proj, conv_w)


DN_WIN = 256
_LEVELS = tuple(1 << i for i in range(DN_CHUNK.bit_length() - 1))


def _dn_constants(reverse):
    n = DN_CHUNK
    r = np.arange(n)[:, None]
    c = np.arange(n)[None, :]
    before = (c > r) if reverse else (c < r)
    incl = before | (r == c)
    lvl = []
    for s in _LEVELS:
        same = (r // (2 * s)) == (c // (2 * s))
        if reverse:
            m = same & ((r % (2 * s)) < s) & ((c % (2 * s)) >= s)
        else:
            m = same & ((r % (2 * s)) >= s) & ((c % (2 * s)) < s)
        lvl.append(m)
    masks = np.stack([before, incl] + lvl).astype(np.float32)
    cum_col = incl.astype(np.float32)
    return masks, cum_col, cum_col.T


def _dn_kernel(meta_ref, qf_ref, kf_ref, vf_ref, smf_ref, qb_ref, kb_ref, vb_ref, smb_ref, s0_ref,
               pa_ref, pdt_ref, msk_ref, cc_ref, cr_ref, of_ref, ob_ref, sfin_ref, s_scr, *, n_ctx_seq):
    j = pl.program_id(0)
    seq = meta_ref[2, j]
    is_ctx = seq < n_ctx_seq

    @pl.when(meta_ref[3, j] == 1)
    def _():
        s_scr[...] = jnp.where(is_ctx, 0.0, s0_ref[...])

    n_chunks = DN_WIN // DN_CHUNK
    dirs = ((0, qf_ref, kf_ref, vf_ref, smf_ref, of_ref), (1, qb_ref, kb_ref, vb_ref, smb_ref, ob_ref))
    ch = {}
    for d, q_ref, k_ref, v_ref, sm_ref, o_ref in dirs:
        sm = sm_ref[...]
        g_all = pa_ref[d] * _softplus(sm + pdt_ref[d])
        beta_all = _sigmoid(sm)
        g_t = g_all.T
        before = msk_ref[d, 0]
        incl = msk_ref[d, 1]
        for c in range(n_chunks):
            rs = slice(c * DN_CHUNK, (c + 1) * DN_CHUNK)
            gc_col = jnp.dot(cc_ref[d], g_all[rs, :], precision=_HI, preferred_element_type=F32)
            gc_row = jnp.dot(g_t[:, rs], cr_ref[d], precision=_HI, preferred_element_type=F32)
            for h in range(DN_HEADS):
                la = 16 + 8 * d + h
                lb = 8 * d + h
                cs = slice(h * DN_DK, (h + 1) * DN_DK)
                gcc = gc_col[:, la:la + 1]
                gcr = gc_row[la:la + 1, :]
                bcol = beta_all[rs, lb:lb + 1]
                dec = jnp.exp(jnp.where(incl > 0, gcc - gcr, NEG_BIG))
                kh = k_ref[rs, cs]
                m = (bcol * _dot_nt(kh, kh)) * (dec * before)
                ch[d, c, h] = dict(d=d, rs=rs, cs=cs, gcc=gcc, bcol=bcol, dec=dec, m=m, o_ref=o_ref,
                                   q_ref=q_ref, k_ref=k_ref, v_ref=v_ref,
                                   t=(incl - before) - m * msk_ref[d, 2])
    chains = list(ch.values())
    for li in range(1, len(_LEVELS)):
        tmp = [_dot(c["t"], c["m"] * msk_ref[c["d"], 2 + li]) for c in chains]
        for c, x in zip(chains, tmp):
            c["t"] = c["t"] - _dot(x, c["t"])
    for c in chains:
        kh = c["k_ref"][c["rs"], c["cs"]]
        egc = jnp.exp(c["gcc"])
        rhs = jnp.concatenate([c["v_ref"][c["rs"], c["cs"]] * c["bcol"], kh * (c["bcol"] * egc)], axis=1)
        c["sol"] = _dot(c["t"], rhs)
        c["egc"] = egc
        c["qk"] = _dot_nt(c["q_ref"][c["rs"], c["cs"]], kh)
    for step in range(n_chunks):
        for d in range(2):
            cidx = step if d == 0 else n_chunks - 1 - step
            last = DN_CHUNK - 1 if d == 0 else 0
            for h in range(DN_HEADS):
                c = ch[d, cidx, h]
                rs, cs, gcc = c["rs"], c["cs"], c["gcc"]
                qh = c["q_ref"][rs, cs]
                kh = c["k_ref"][rs, cs]
                u = c["sol"][:, :DN_DK]
                w = c["sol"][:, DN_DK:]
                s = s_scr[d, h]
                v_new = u - _dot(w, s)
                c["o_ref"][rs, cs] = _dot(qh * c["egc"], s) + _dot(c["qk"] * c["dec"], v_new)
                gl = gcc[last:last + 1, :]
                s_scr[d, h] = s * jnp.exp(gl) + _dot_tn(kh * jnp.exp(gl - gcc), v_new)

    @pl.when((meta_ref[4, j] == 1) & is_ctx)
    def _():
        sfin_ref[...] = s_scr[...]


def _deltanet(qkv, gates, state0, a_log, dt_bias, geom):
    n_tok = qkv.shape[0]
    width = DN_HEADS * DN_DK
    rows = []
    for b in range(geom.b_ctx + geom.b_lat):
        is_ctx = b < geom.b_ctx
        t_seq = geom.t_ctx if is_ctx else geom.t_lat
        base = (b * geom.t_ctx if is_ctx else geom.n_ctx + (b - geom.b_ctx) * geom.t_lat) // DN_WIN
        n_win = t_seq // DN_WIN
        for w in range(n_win):
            rows.append((base + w, base + n_win - 1 - w, b, int(w == 0), int(w == n_win - 1)))
    meta = jnp.asarray(np.array(rows, np.int32).T)
    n_steps = len(rows)

    consts = [_dn_constants(False), _dn_constants(True)]
    masks = jnp.asarray(np.stack([c[0] for c in consts]))
    cum_col = jnp.asarray(np.stack([c[1] for c in consts]))
    cum_row = jnp.asarray(np.stack([c[2] for c in consts]))
    lanes = jnp.arange(LANE)
    pa, pdt = [], []
    for d in range(2):
        hd = jnp.clip(lanes - (16 + 8 * d), 0, DN_HEADS - 1)
        valid = (lanes >= 16 + 8 * d) & (lanes < 24 + 8 * d)
        pa.append(jnp.where(valid, -jnp.exp(a_log[d].astype(F32))[hd], 0.0))
        pdt.append(jnp.where(valid, dt_bias[d].astype(F32)[hd], 0.0))
    pa = jnp.stack(pa).reshape(2, 1, LANE)
    pdt = jnp.stack(pdt).reshape(2, 1, LANE)

    one = pl.Buffered(1)
    n_ctx_seq = geom.b_ctx
    st_shape = (2, DN_HEADS, DN_DK, DN_DK)

    def win(row, col):
        return pl.BlockSpec((DN_WIN, width if col < 3 else LANE), lambda j, meta: (meta[row, j], col % 3))

    def const(a):
        nd = a.ndim
        return pl.BlockSpec(a.shape, lambda j, meta: (0,) * nd, pipeline_mode=one)

    grid_spec = pltpu.PrefetchScalarGridSpec(
        num_scalar_prefetch=1,
        grid=(n_steps,),
        in_specs=[win(0, 0), win(0, 1), win(0, 2), win(0, 3), win(1, 0), win(1, 1), win(1, 2), win(1, 3),
                  pl.BlockSpec((None,) + st_shape, lambda j, meta: (jnp.maximum(meta[2, j] - n_ctx_seq, 0), 0, 0, 0, 0)),
                  const(pa), const(pdt), const(masks), const(cum_col), const(cum_row)],
        out_specs=[pl.BlockSpec((DN_WIN, width), lambda j, meta: (meta[0, j], 0)),
                   pl.BlockSpec((DN_WIN, width), lambda j, meta: (meta[1, j], 0)),
                   pl.BlockSpec((None,) + st_shape, lambda j, meta: (jnp.minimum(meta[2, j], n_ctx_seq - 1), 0, 0, 0, 0))],
        scratch_shapes=[pltpu.VMEM(st_shape, F32)],
    )
    return pl.pallas_call(
        functools.partial(_dn_kernel, n_ctx_seq=n_ctx_seq),
        grid_spec=grid_spec,
        out_shape=[jax.ShapeDtypeStruct((n_tok, width), F32), jax.ShapeDtypeStruct((n_tok, width), F32),
                   jax.ShapeDtypeStruct((n_ctx_seq,) + st_shape, F32)],
        compiler_params=_cparams(("arbitrary",)),
        name="deltanet",
    )(meta, qkv, qkv, qkv, gates, qkv, qkv, qkv, gates, state0, pa, pdt, masks, cum_col, cum_row)


def _gmlp_kernel(u_ref, vg_ref, nw_ref, ws_ref, bias_ref, o_ref, *, tb):
    vg = _gelu(vg_ref[...])
    ms = jnp.mean(vg * vg, axis=-1, keepdims=True)
    vn = (vg * lax.rsqrt(ms + EPS) * nw_ref[...]).astype(BF16)
    for ch in range(tb // GM_CHUNK):
        rs = slice(ch * GM_CHUNK, (ch + 1) * GM_CHUNK)
        for g in range(GM_GROUPS):
            cs = slice(g * GM_CH, (g + 1) * GM_CH)
            mixed = jnp.dot(ws_ref[g], vn[rs, cs], preferred_element_type=F32) + bias_ref[g]
            o_ref[rs, cs] = (_gelu(u_ref[rs, cs]) * mixed).astype(BF16)


def _gmlp(proj, gm_norm, w_s, b_s, u_blk, vg_blk):
    n_tok = proj.shape[0]
    width = GM_GROUPS * GM_CH
    tb = 256
    bias = jnp.broadcast_to(b_s.astype(F32)[:, :, None], (GM_GROUPS, GM_CHUNK, GM_CH))
    return pl.pallas_call(
        functools.partial(_gmlp_kernel, tb=tb),
        grid=(n_tok // tb,),
        in_specs=[pl.BlockSpec((tb, width), lambda i: (i, u_blk)),
                  pl.BlockSpec((tb, width), lambda i: (i, vg_blk)),
                  pl.BlockSpec((1, width), lambda i: (0, 0)),
                  pl.BlockSpec((GM_GROUPS, GM_CHUNK, GM_CHUNK), lambda i: (0, 0, 0)),
                  pl.BlockSpec((GM_GROUPS, GM_CHUNK, GM_CH), lambda i: (0, 0, 0))],
        out_specs=pl.BlockSpec((tb, width), lambda i: (i, 0)),
        out_shape=jax.ShapeDtypeStruct((n_tok, width), BF16),
        compiler_params=_cparams(("parallel",)),
        name="gmlp_gate",
    )(proj, proj, gm_norm.reshape(1, width), w_s.astype(BF16), bias)


def _merge_kernel(of_ref, ob_ref, z_ref, sb_ref, ga_ref, gb_ref, dnw_ref, wa_ref, wb_ref, o_ref, on_scr):
    for h in range(DN_HEADS):
        cs = slice(h * DN_DK, (h + 1) * DN_DK)
        o = of_ref[:, cs] + ob_ref[:, cs]
        ms = jnp.mean(o * o, axis=-1, keepdims=True)
        z = z_ref[:, cs]
        on_scr[:, cs] = (o * lax.rsqrt(ms + EPS) * dnw_ref[...] * (z * _sigmoid(z))).astype(BF16)
    p_a = jnp.dot(on_scr[...], wa_ref[...], preferred_element_type=F32)
    p_b = jnp.dot(sb_ref[...], wb_ref[...], preferred_element_type=F32)
    o_ref[...] = (_sigmoid(ga_ref[...]) * p_a + _sigmoid(gb_ref[...]) * p_b).astype(BF16)


def _merge(o_f, o_b, proj, s_b, dn_norm, w_a, w_b, z_blk, ga_blk, gb_blk):
    n_tok = proj.shape[0]
    wa_in, D = w_a.shape
    wb_in = w_b.shape[0]
    tb = 256
    return pl.pallas_call(
        _merge_kernel,
        grid=(n_tok // tb,),
        in_specs=[pl.BlockSpec((tb, wa_in), lambda i: (i, 0)),
                  pl.BlockSpec((tb, wa_in), lambda i: (i, 0)),
                  pl.BlockSpec((tb, wa_in), lambda i: (i, z_blk)),
                  pl.BlockSpec((tb, wb_in), lambda i: (i, 0)),
                  pl.BlockSpec((tb, D), lambda i: (i, ga_blk)),
                  pl.BlockSpec((tb, D), lambda i: (i, gb_blk)),
                  pl.BlockSpec((1, DN_DK), lambda i: (0, 0)),
                  pl.BlockSpec((wa_in, D), lambda i: (0, 0)),
                  pl.BlockSpec((wb_in, D), lambda i: (0, 0))],
        out_specs=pl.BlockSpec((tb, D), lambda i: (i, 0)),
        out_shape=jax.ShapeDtypeStruct((n_tok, D), BF16),
        scratch_shapes=[pltpu.VMEM((tb, wa_in), BF16)],
        compiler_params=_cparams(("parallel",)),
        name="branch_merge",
    )(o_f, o_b, proj, s_b, proj, proj, dn_norm.reshape(1, DN_DK), w_a.astype(BF16), w_b.astype(BF16))


def _outproj_kernel(m_ref, w_ref, x_ref, gt_ref, o_ref):
    y = jnp.dot(m_ref[...], w_ref[...], preferred_element_type=F32)
    o_ref[...] = x_ref[...] + gt_ref[...] * y


def _out_projection(merged, w_out, x, mod, geom):
    n_tok, D = x.shape
    tb = geom.block(512)
    return pl.pallas_call(
        _outproj_kernel,
        grid=(n_tok // tb,),
        in_specs=[pl.BlockSpec((tb, D), lambda i: (i, 0)),
                  pl.BlockSpec((D, D), lambda i: (0, 0)),
                  pl.BlockSpec((tb, D), lambda i: (i, 0)),
                  _mod_block(geom, tb, 2, D, 1)],
        out_specs=pl.BlockSpec((tb, D), lambda i: (i, 0)),
        out_shape=jax.ShapeDtypeStruct((n_tok, D), F32),
        compiler_params=_cparams(("parallel",)),
        name="out_proj_residual",
    )(merged, w_out.astype(BF16), x, mod)


def _top_rows(cur, k):
    vals = []
    for _ in range(k):
        m = jnp.max(cur, axis=0, keepdims=True)
        vals.append(m)
        cur = jnp.where(cur == m, -jnp.inf, cur)
    return vals


_PAIRS = [(a, b) for a in range(PEER_TOPK) for b in range(PEER_TOPK) if (a + 1) * (b + 1) <= PEER_TOPK + 1]


def _peer_select_kernel(x_ref, nw_ref, sc_ref, sh_ref, wq_ref, keys_ref,
                        ht_ref, thr_ref, s1_ref, a_ref, b_ref):
    x = x_ref[...]
    ms = jnp.mean(x * x, axis=-1, keepdims=True)
    y = x * lax.rsqrt(ms + EPS) * nw_ref[...]
    h2_t = (y * (1.0 + sc_ref[...]) + sh_ref[...]).T.astype(BF16)
    ht_ref[...] = h2_t
    q_t = jnp.dot(wq_ref[...], h2_t, preferred_element_type=F32)
    dk = PEER_NKEYS
    for h in range(PEER_HEADS):
        svs = []
        scores = []
        for p in range(2):
            r0 = (h * 2 + p) * dk
            s = jnp.dot(keys_ref[h * 2 + p], q_t[r0:r0 + dk, :].astype(BF16), preferred_element_type=F32)
            svs.append(_top_rows(s, PEER_TOPK))
            scores.append(s)
        cand = jnp.concatenate([svs[0][a] + svs[1][b] for a, b in _PAIRS], axis=0)
        best = _top_rows(cand, PEER_TOPK + 1)
        m0 = svs[0][0]
        m1 = svs[1][0]
        mx = best[0]
        z = jnp.zeros_like(mx)
        for v in best[:PEER_TOPK]:
            z = z + jnp.exp(v - mx)
        theta = 0.5 * (best[PEER_TOPK - 1] + best[PEER_TOPK])
        thr_ref[h] = theta - scores[0]
        s1_ref[h] = scores[1]
        a_ref[h] = jnp.exp(scores[0] - m0)
        b_ref[h] = jnp.exp(scores[1] - m1) * (0.5 * jnp.exp(m0 + m1 - mx) / z)


def _peer_select(x, norm_w, mod, wq_t, keys, geom):
    n_tok, D = x.shape
    tb = geom.block(256)
    H, NK = PEER_HEADS, PEER_NKEYS
    big = jax.ShapeDtypeStruct((H, NK, n_tok), F32)
    big_spec = pl.BlockSpec((H, NK, tb), lambda i: (0, 0, i))
    return pl.pallas_call(
        _peer_select_kernel,
        grid=(n_tok // tb,),
        in_specs=[pl.BlockSpec((tb, D), lambda i: (i, 0)),
                  pl.BlockSpec((1, D), lambda i: (0, 0)),
                  _mod_block(geom, tb, 4, D, 1),
                  _mod_block(geom, tb, 3, D, 1),
                  pl.BlockSpec(wq_t.shape, lambda i: (0, 0)),
                  pl.BlockSpec(keys.shape, lambda i: (0, 0, 0))],
        out_specs=[pl.BlockSpec((D, tb), lambda i: (0, i)), big_spec, big_spec, big_spec, big_spec],
        out_shape=[jax.ShapeDtypeStruct((D, n_tok), BF16), big, big, big, big],
        compiler_params=_cparams(("parallel",)),
        name="peer_select",
    )(x, norm_w.reshape(1, D), mod, mod, wq_t, keys)


PEER_EBLK = 1024
PEER_TBLK = 512
_PD_ROWS = 64
_PD_IGRP = 4
_GELU_C1 = math.sqrt(2.0 / math.pi)
_GELU_C2 = _GELU_C1 * 0.044715


def _peer_dense_kernel(ht_ref, u_ref, vt_ref, thr_ref, a_ref, s1_ref, b_ref, x_ref, gt_ref, *rest, n_e, tb, n_ctx_blk):
    if n_ctx_blk is None:
        o_ref, acc_scr, act0, act1, w0, w1 = rest
    else:
        fw_ref, oc_ref, ol_ref, acc_scr, act0, act1, w0, w1 = rest
    s = pl.program_id(0)
    e_c = jnp.maximum(s - 2, 0) % n_e

    @pl.when(s == 0)
    def _():
        for r in (act0, act1, w0, w1):
            r[...] = jnp.zeros_like(r)

    @pl.when(e_c == 0)
    def _():
        acc_scr[...] = jnp.zeros_like(acc_scr)

    def stages(act_w, act_r, w_w, w_r):
        act_w[...] = jnp.dot(u_ref[...], ht_ref[...], preferred_element_type=F32)
        acc_scr[...] += jnp.dot(vt_ref[...], w_r[...], preferred_element_type=F32)
        nk = PEER_NKEYS
        for tc in range(tb // LANE):
            cs = slice(tc * LANE, (tc + 1) * LANE)
            for jh in range(nk // _PD_ROWS):
                js = slice(jh * _PD_ROWS, (jh + 1) * _PD_ROWS)
                for ig in range(PEER_EBLK // nk // _PD_IGRP):
                    g = [None] * _PD_IGRP
                    for h in range(PEER_HEADS):
                        s1t = s1_ref[h, js, cs]
                        bt = b_ref[h, js, cs]
                        for k in range(_PD_IGRP):
                            ii = ig * _PD_IGRP + k
                            sel = s1t >= thr_ref[h, ii:ii + 1, cs]
                            gh = jnp.where(sel, bt * a_ref[h, ii:ii + 1, cs], 0.0)
                            g[k] = gh if g[k] is None else g[k] + gh
                    for k in range(_PD_IGRP):
                        r0 = (ig * _PD_IGRP + k) * nk + jh * _PD_ROWS
                        xa = act_r[r0:r0 + _PD_ROWS, cs]
                        tt = jnp.tanh(xa * (_GELU_C1 + _GELU_C2 * (xa * xa)))
                        w_w[r0:r0 + _PD_ROWS, cs] = ((xa + xa * tt) * g[k]).astype(BF16)

    @pl.when(s % 2 == 0)
    def _():
        stages(act0, act1, w1, w0)

    @pl.when(s % 2 == 1)
    def _():
        stages(act1, act0, w0, w1)

    done = (e_c == n_e - 1) & (s >= 2)
    if n_ctx_blk is None:
        @pl.when(done)
        def _():
            o_ref[...] = x_ref[...] + gt_ref[...] * acc_scr[...].T
    else:
        t_c = jnp.maximum(s - 2, 0) // n_e

        def normed():
            xn = x_ref[...] + gt_ref[...] * acc_scr[...].T
            ms = jnp.mean(xn * xn, axis=-1, keepdims=True)
            return xn * lax.rsqrt(ms + EPS) * fw_ref[...]

        @pl.when(done & (t_c < n_ctx_blk))
        def _():
            oc_ref[...] = normed()

        @pl.when(done & (t_c >= n_ctx_blk))
        def _():
            ol_ref[...] = normed()


def _peer_dense(h2_t, u_bf16, vt_bf16, thr, s1, a, b_half, x, mod, geom, final_w=None):
    n_tok, D = x.shape
    n_exp = u_bf16.shape[0]
    tb = geom.block(PEER_TBLK)
    n_e = n_exp // PEER_EBLK
    n_steps = (n_tok // tb) * n_e
    H, NK = PEER_HEADS, PEER_NKEYS
    ib = PEER_EBLK // NK
    row = geom.mod_row(tb)
    one = pl.Buffered(1)
    n_ctx_blk = geom.n_ctx // tb

    def step(s, lag):
        return jnp.clip(s - lag, 0, n_steps - 1)

    def tok(s, lag):
        return step(s, lag) // n_e

    def exp(s, lag):
        return step(s, lag) % n_e

    in_specs = [pl.BlockSpec((D, tb), lambda s: (0, tok(s, 0))),
                pl.BlockSpec((PEER_EBLK, D), lambda s: (exp(s, 0), 0)),
                pl.BlockSpec((D, PEER_EBLK), lambda s: (0, exp(s, 2))),
                pl.BlockSpec((H, ib, tb), lambda s: (0, exp(s, 1), tok(s, 1))),
                pl.BlockSpec((H, ib, tb), lambda s: (0, exp(s, 1), tok(s, 1))),
                pl.BlockSpec((H, NK, tb), lambda s: (0, 0, tok(s, 1)), pipeline_mode=one),
                pl.BlockSpec((H, NK, tb), lambda s: (0, 0, tok(s, 1)), pipeline_mode=one),
                pl.BlockSpec((tb, D), lambda s: (tok(s, 2), 0), pipeline_mode=one),
                pl.BlockSpec((None, None, 1, D), lambda s: (row(tok(s, 2)), 5, 0, 0))]
    args = [h2_t, u_bf16, vt_bf16, thr, a, s1, b_half, x, mod]
    if final_w is None:
        out_specs = pl.BlockSpec((tb, D), lambda s: (tok(s, 2), 0))
        out_shape = jax.ShapeDtypeStruct((n_tok, D), F32)
    else:
        in_specs.append(pl.BlockSpec((1, D), lambda s: (0, 0)))
        args.append(final_w.reshape(1, D))
        out_specs = [pl.BlockSpec((tb, D), lambda s: (jnp.minimum(tok(s, 2), n_ctx_blk - 1), 0), pipeline_mode=one),
                     pl.BlockSpec((tb, D), lambda s: (jnp.maximum(tok(s, 2) - n_ctx_blk, 0), 0), pipeline_mode=one)]
        out_shape = [jax.ShapeDtypeStruct((geom.n_ctx, D), F32), jax.ShapeDtypeStruct((geom.n_lat, D), F32)]
    return pl.pallas_call(
        functools.partial(_peer_dense_kernel, n_e=n_e, tb=tb, n_ctx_blk=None if final_w is None else n_ctx_blk),
        grid=(n_steps + 2,),
        in_specs=in_specs,
        out_specs=out_specs,
        out_shape=out_shape,
        scratch_shapes=[pltpu.VMEM((D, tb), F32),
                        pltpu.VMEM((PEER_EBLK, tb), F32), pltpu.VMEM((PEER_EBLK, tb), F32),
                        pltpu.VMEM((PEER_EBLK, tb), BF16), pltpu.VMEM((PEER_EBLK, tb), BF16)],
        compiler_params=_cparams(("arbitrary",)),
        name="peer_dense",
    )(*args)


def _grid_pos_embed(T, D):
    rows = T // GRID_W
    r, col = jnp.meshgrid(jnp.arange(rows, dtype=F32), jnp.arange(GRID_W, dtype=F32), indexing='ij')
    quarter = D // 4
    freq = 1.0 / (10000.0 ** (jnp.arange(quarter, dtype=F32) / quarter))

    def enc(p):
        ang = p.reshape(-1, 1) * freq
        return jnp.concatenate([jnp.sin(ang), jnp.cos(ang)], axis=-1)

    return jnp.concatenate([enc(r), enc(col)], axis=-1)


def kernel(x_prompt, x_sample, state_delta, c, c_ctx, w_mod, b_mod, norm_mix, w_in, conv_w, dn_a_log,
           dn_dt_bias, dn_norm, gm_norm, gm_w_s, gm_b_s, w_branch_a, w_branch_b, w_out, norm_ffn,
           peer_w_query, peer_sub_keys, peer_u, peer_v, norm_final):
    b_ctx, t_ctx, D = x_prompt.shape
    b_lat, t_lat, _ = x_sample.shape
    depth = w_mod.shape[0]
    geom = _Geom(b_ctx, t_ctx, b_lat, t_lat)
    qk_w = DN_HEADS * DN_DK
    gm_w = GM_GROUPS * GM_CH
    assert t_ctx % CONV_BLK == 0 and t_lat % CONV_BLK == 0 and qk_w == gm_w == PROJ_BLK and D == 2 * qk_w

    xs = x_sample + _grid_pos_embed(t_lat, D).astype(x_sample.dtype)[None]
    x = jnp.concatenate([x_prompt.reshape(geom.n_ctx, D), xs.reshape(geom.n_lat, D)], axis=0)

    n_rows = -(-(1 + b_lat) // SUBLANE) * SUBLANE
    cond = jnp.concatenate([c_ctx[None, :], c, jnp.zeros((n_rows - 1 - b_lat, D), c.dtype)], axis=0)
    mod_all = _modulation(cond, w_mod, b_mod).reshape(depth, n_rows, N_MOD, 1, D)

    z_blk, u_blk, vg_blk = 3, 4, 5
    ga_blk, gb_blk = 3, 4

    new_states = []
    for l in range(depth):
        mod = mod_all[l]
        proj, gates = _in_projection(x, norm_mix[l], mod, w_in[l], geom)
        qkv = _conv_qkv(proj, conv_w[l], geom)
        o_f, o_b, s_ctx = _deltanet(qkv, gates, state_delta[:, l], dn_a_log[l], dn_dt_bias[l], geom)
        new_states.append(s_ctx)
        s_b = _gmlp(proj, gm_norm[l], gm_w_s[l], gm_b_s[l], u_blk, vg_blk)
        merged = _merge(o_f, o_b, proj, s_b, dn_norm[l], w_branch_a[l], w_branch_b[l], z_blk, ga_blk, gb_blk)
        x = _out_projection(merged, w_out[l], x, mod, geom)
        wq_t = peer_w_query[l].reshape(D, -1).T.astype(BF16)
        keys = peer_sub_keys[l].reshape(PEER_HEADS * 2, PEER_NKEYS, -1).astype(BF16)
        h2_t, thr, s1, a, b = _peer_select(x, norm_ffn[l], mod, wq_t, keys, geom)
        x = _peer_dense(h2_t, peer_u[l].astype(BF16), peer_v[l].T.astype(BF16), thr, s1, a, b, x, mod, geom,
                        final_w=norm_final if l == depth - 1 else None)

    y_prompt = x[0].reshape(b_ctx, t_ctx, D)
    y_sample = x[1].reshape(b_lat, t_lat, D)
    new_state_delta = jnp.stack(new_states, axis=1).astype(x_prompt.dtype)
    return (y_prompt, y_sample, new_state_delta)
```

```python
import functools
import math

import numpy as np
import jax
import jax.numpy as jnp
from jax import lax
from jax.experimental import pallas as pl
from jax.experimental.pallas import tpu as pltpu

F32 = jnp.float32
BF16 = jnp.bfloat16

EPS = 1e-6
N_MOD = 6
GRID_W = 64
DN_HEADS = 8
DN_DK = 128
DN_CHUNK = 128
GM_GROUPS = 8
GM_CH = 128
GM_CHUNK = 128
PEER_HEADS = 8
PEER_NKEYS = 128
PEER_TOPK = 16
LANE = 128
SUBLANE = 8
VMEM_LIMIT = 56 * 1024 * 1024
NEG_BIG = -1e30

_HI = lax.Precision.HIGHEST


def _cparams(sem):
    return pltpu.CompilerParams(dimension_semantics=sem, vmem_limit_bytes=VMEM_LIMIT)


def _pick(n, cap):
    b = 1
    while b * 2 <= cap and n % (b * 2) == 0:
        b *= 2
    return b


def _dot(a, b):
    return jnp.dot(a.astype(BF16), b.astype(BF16), preferred_element_type=F32)


def _dot_nt(a, b):
    return lax.dot_general(a.astype(BF16), b.astype(BF16), (((1,), (1,)), ((), ())),
                           preferred_element_type=F32)


def _dot_tn(a, b):
    return lax.dot_general(a.astype(BF16), b.astype(BF16), (((0,), (0,)), ((), ())),
                           preferred_element_type=F32)


def _gelu(x):
    return 0.5 * x * (1.0 + jnp.tanh(math.sqrt(2.0 / math.pi) * (x + 0.044715 * (x * x * x))))


def _sigmoid(x):
    return 1.0 / (1.0 + jnp.exp(-x))


def _softplus(x):
    return jnp.maximum(x, 0.0) + jnp.log(1.0 + jnp.exp(-jnp.abs(x)))


def _mod_kernel(c_ref, w_ref, b_ref, o_ref):
    c = c_ref[...]
    o_ref[...] = _dot(c * _sigmoid(c), w_ref[...]) + b_ref[...]


def _modulation(cond, w_mod, b_mod):
    L, D, N = w_mod.shape
    R = cond.shape[0]
    tn = _pick(N, 1024)
    return pl.pallas_call(
        _mod_kernel,
        grid=(L, N // tn),
        in_specs=[pl.BlockSpec((R, D), lambda l, j: (0, 0)),
                  pl.BlockSpec((None, D, tn), lambda l, j: (l, 0, j)),
                  pl.BlockSpec((None, 1, tn), lambda l, j: (l, 0, j))],
        out_specs=pl.BlockSpec((None, R, tn), lambda l, j: (l, 0, j)),
        out_shape=jax.ShapeDtypeStruct((L, R, N), F32),
        compiler_params=_cparams(("parallel", "parallel")),
        name="adaln_mod",
    )(cond, w_mod, b_mod.reshape(L, 1, N))


class _Geom:
    def __init__(self, b_ctx, t_ctx, b_lat, t_lat):
        self.b_ctx, self.t_ctx, self.b_lat, self.t_lat = b_ctx, t_ctx, b_lat, t_lat
        self.n_ctx = b_ctx * t_ctx
        self.n_lat = b_lat * t_lat
        self.n_tok = self.n_ctx + self.n_lat

    def block(self, cap):
        return _pick(math.gcd(self.n_ctx, self.t_lat), cap)

    def mod_row(self, tb):
        n_ctx_blocks = self.n_ctx // tb
        per_seq = self.t_lat // tb

        def row(i):
            return jnp.where(i < n_ctx_blocks, 0, 1 + (i - n_ctx_blocks) // per_seq)

        return row


def _mod_block(geom, tb, which, D, ngrid):
    row = geom.mod_row(tb)
    if ngrid == 1:
        return pl.BlockSpec((None, None, 1, D), lambda i: (row(i), which, 0, 0))
    return pl.BlockSpec((None, None, 1, D), lambda i, j: (row(i), which, 0, 0))


PROJ_BLK = 1024
N_GATES = 4 * DN_HEADS


def _inproj_kernel(x_ref, nw_ref, sc_ref, sh_ref, wa_ref, wb_ref, ws_ref, o_ref, sm_ref, h_scr, *, n_a):
    j = pl.program_id(1)

    @pl.when(j == 0)
    def _():
        x = x_ref[...]
        ms = jnp.mean(x * x, axis=-1, keepdims=True)
        y = x * lax.rsqrt(ms + EPS) * nw_ref[...]
        h_scr[...] = (y * (1.0 + sc_ref[...]) + sh_ref[...]).astype(BF16)
        sm_ref[...] = jnp.dot(h_scr[...], ws_ref[...], preferred_element_type=F32)

    @pl.when(j < n_a)
    def _():
        o_ref[...] = jnp.dot(h_scr[...], wa_ref[...], preferred_element_type=F32)

    @pl.when(j >= n_a)
    def _():
        o_ref[...] = jnp.dot(h_scr[...], wb_ref[...], preferred_element_type=F32)


def _in_projection(x, norm_w, mod, w_in_l, geom):
    n_tok, D = x.shape
    w = w_in_l.astype(BF16)
    o_small = 4 * PROJ_BLK
    n_a = o_small // PROJ_BLK
    w_b = w[:, o_small + N_GATES:]
    n_b = w_b.shape[1] // PROJ_BLK
    w_s = jnp.pad(w[:, o_small:o_small + N_GATES], ((0, 0), (0, LANE - N_GATES)))
    tb = geom.block(1024)
    return pl.pallas_call(
        functools.partial(_inproj_kernel, n_a=n_a),
        grid=(n_tok // tb, n_a + n_b),
        in_specs=[pl.BlockSpec((tb, D), lambda i, j: (i, 0)),
                  pl.BlockSpec((1, D), lambda i, j: (0, 0)),
                  _mod_block(geom, tb, 1, D, 2),
                  _mod_block(geom, tb, 0, D, 2),
                  pl.BlockSpec((D, PROJ_BLK), lambda i, j: (0, jnp.minimum(j, n_a - 1))),
                  pl.BlockSpec((D, PROJ_BLK), lambda i, j: (0, jnp.maximum(j - n_a, 0))),
                  pl.BlockSpec((D, LANE), lambda i, j: (0, 0))],
        out_specs=[pl.BlockSpec((tb, PROJ_BLK), lambda i, j: (i, j)),
                   pl.BlockSpec((tb, LANE), lambda i, j: (i, 0))],
        out_shape=[jax.ShapeDtypeStruct((n_tok, (n_a + n_b) * PROJ_BLK), F32),
                   jax.ShapeDtypeStruct((n_tok, LANE), F32)],
        scratch_shapes=[pltpu.VMEM((tb, D), BF16)],
        compiler_params=_cparams(("parallel", "arbitrary")),
        name="norm_in_proj",
    )(x, norm_w.reshape(1, D), mod, mod, w, w_b, w_s)


CONV_BLK = 256


CONV_ROWS = 64


def _conv_kernel(cur_ref, prev_ref, next_ref, w_ref, o_ref, *, geom):
    i = pl.program_id(0)
    j = pl.program_id(1)
    n_ctx_blocks = geom.n_ctx // CONV_BLK
    bps_ctx = geom.t_ctx // CONV_BLK
    bps_lat = geom.t_lat // CONV_BLK
    pos = jnp.where(i < n_ctx_blocks, i % bps_ctx, (i - n_ctx_blocks) % bps_lat)
    per = jnp.where(i < n_ctx_blocks, bps_ctx, bps_lat)
    before_blk = jnp.where(pos == 0, 0.0, prev_ref[SUBLANE - 1:SUBLANE, :])
    after_blk = jnp.where(pos == per - 1, 0.0, next_ref[0:1, :])
    w = w_ref[...]
    n_chunks = CONV_BLK // CONV_ROWS
    rid = lax.broadcasted_iota(jnp.int32, (CONV_ROWS, w.shape[1]), 0)

    def conv_silu(c):
        r0 = c * CONV_ROWS
        x = cur_ref[r0:r0 + CONV_ROWS, :]
        row_before = before_blk if c == 0 else cur_ref[r0 - 1:r0, :]
        row_after = after_blk if c == n_chunks - 1 else cur_ref[r0 + CONV_ROWS:r0 + CONV_ROWS + 1, :]
        xm1 = jnp.where(rid == 0, row_before, pltpu.roll(x, 1, axis=0))
        xp1 = jnp.where(rid == CONV_ROWS - 1, row_after, pltpu.roll(x, CONV_ROWS - 1, axis=0))
        y = w[0:1, :] * xm1 + w[1:2, :] * x + w[2:3, :] * xp1
        return y * _sigmoid(y)

    @pl.when(j == 2)
    def _():
        for c in range(n_chunks):
            o_ref[c * CONV_ROWS:(c + 1) * CONV_ROWS, :] = conv_silu(c).astype(BF16)

    @pl.when(j < 2)
    def _():
        scale = jnp.where(j == 0, DN_DK ** -0.5, 1.0).astype(F32)
        for c in range(n_chunks):
            y = conv_silu(c)
            for h in range(DN_HEADS):
                seg = y[:, h * DN_DK:(h + 1) * DN_DK]
                ss = jnp.sum(seg * seg, axis=-1, keepdims=True)
                o_ref[c * CONV_ROWS:(c + 1) * CONV_ROWS, h * DN_DK:(h + 1) * DN_DK] = (
                    seg * (lax.rsqrt(ss + EPS) * scale)).astype(BF16)


def _conv_qkv(proj, conv_w, geom):
    n_tok = proj.shape[0]
    width = DN_HEADS * DN_DK
    nb8 = n_tok // SUBLANE
    r = CONV_BLK // SUBLANE
    return pl.pallas_call(
        functools.partial(_conv_kernel, geom=geom),
        grid=(n_tok // CONV_BLK, 3),
        in_specs=[pl.BlockSpec((CONV_BLK, width), lambda i, j: (i, j)),
                  pl.BlockSpec((SUBLANE, width), lambda i, j: (jnp.maximum(i * r - 1, 0), j)),
                  pl.BlockSpec((SUBLANE, width), lambda i, j: (jnp.minimum((i + 1) * r, nb8 - 1), j)),
                  pl.BlockSpec((3, width), lambda i, j: (0, j))],
        out_specs=pl.BlockSpec((CONV_BLK, width), lambda i, j: (i, j)),
        out_shape=jax.ShapeDtypeStruct((n_tok, 3 * width), BF16),
        compiler_params=_cparams(("parallel", "parallel")),
        name="conv_silu_l2",
    )(proj, proj, proj, conv_w)


DN_WIN = 256
_LEVELS = tuple(1 << i for i in range(DN_CHUNK.bit_length() - 1))


def _dn_constants(reverse):
    n = DN_CHUNK
    r = np.arange(n)[:, None]
    c = np.arange(n)[None, :]
    before = (c > r) if reverse else (c < r)
    incl = before | (r == c)
    lvl = []
    for s in _LEVELS:
        same = (r // (2 * s)) == (c // (2 * s))
        if reverse:
            m = same & ((r % (2 * s)) < s) & ((c % (2 * s)) >= s)
        else:
            m = same & ((r % (2 * s)) >= s) & ((c % (2 * s)) < s)
        lvl.append(m)
    masks = np.stack([before, incl] + lvl).astype(np.float32)
    cum_col = incl.astype(np.float32)
    return masks, cum_col, cum_col.T


def _dn_kernel(meta_ref, qf_ref, kf_ref, vf_ref, smf_ref, qb_ref, kb_ref, vb_ref, smb_ref, s0_ref,
               pa_ref, pdt_ref, msk_ref, cc_ref, cr_ref, of_ref, ob_ref, sfin_ref, s_scr, *, n_ctx_seq):
    j = pl.program_id(0)
    seq = meta_ref[2, j]
    is_ctx = seq < n_ctx_seq

    @pl.when(meta_ref[3, j] == 1)
    def _():
        s_scr[...] = jnp.where(is_ctx, 0.0, s0_ref[...])

    n_chunks = DN_WIN // DN_CHUNK
    dirs = ((0, qf_ref, kf_ref, vf_ref, smf_ref, of_ref), (1, qb_ref, kb_ref, vb_ref, smb_ref, ob_ref))
    ch = {}
    for d, q_ref, k_ref, v_ref, sm_ref, o_ref in dirs:
        sm = sm_ref[...]
        g_all = pa_ref[d] * _softplus(sm + pdt_ref[d])
        beta_all = _sigmoid(sm)
        g_t = g_all.T
        before = msk_ref[d, 0]
        incl = msk_ref[d, 1]
        for c in range(n_chunks):
            rs = slice(c * DN_CHUNK, (c + 1) * DN_CHUNK)
            gc_col = jnp.dot(cc_ref[d], g_all[rs, :], precision=_HI, preferred_element_type=F32)
            gc_row = jnp.dot(g_t[:, rs], cr_ref[d], precision=_HI, preferred_element_type=F32)
            for h in range(DN_HEADS):
                la = 16 + 8 * d + h
                lb = 8 * d + h
                cs = slice(h * DN_DK, (h + 1) * DN_DK)
                gcc = gc_col[:, la:la + 1]
                gcr = gc_row[la:la + 1, :]
                bcol = beta_all[rs, lb:lb + 1]
                dec = jnp.exp(jnp.where(incl > 0, gcc - gcr, NEG_BIG))
                kh = k_ref[rs, cs]
                m = (bcol * _dot_nt(kh, kh)) * (dec * before)
                ch[d, c, h] = dict(d=d, rs=rs, cs=cs, gcc=gcc, bcol=bcol, dec=dec, m=m, o_ref=o_ref,
                                   q_ref=q_ref, k_ref=k_ref, v_ref=v_ref,
                                   t=(incl - before) - m * msk_ref[d, 2])
    chains = list(ch.values())
    for li in range(1, len(_LEVELS)):
        tmp = [_dot(c["t"], c["m"] * msk_ref[c["d"], 2 + li]) for c in chains]
        for c, x in zip(chains, tmp):
            c["t"] = c["t"] - _dot(x, c["t"])
    for c in chains:
        kh = c["k_ref"][c["rs"], c["cs"]]
        egc = jnp.exp(c["gcc"])
        rhs = jnp.concatenate([c["v_ref"][c["rs"], c["cs"]] * c["bcol"], kh * (c["bcol"] * egc)], axis=1)
        c["sol"] = _dot(c["t"], rhs)
        c["egc"] = egc
        c["qk"] = _dot_nt(c["q_ref"][c["rs"], c["cs"]], kh)
    pairs = [(d, h) for d in range(2) for h in range(DN_HEADS)]
    state = {p: s_scr[p[0], p[1]] for p in pairs}
    for step in range(n_chunks):
        cur = {(d, h): ch[d, step if d == 0 else n_chunks - 1 - step, h] for d, h in pairs}
        w_s = {p: _dot(cur[p]["sol"][:, DN_DK:], state[p]) for p in pairs}
        q_s = {p: _dot(cur[p]["q_ref"][cur[p]["rs"], cur[p]["cs"]] * cur[p]["egc"], state[p]) for p in pairs}
        v_new = {p: cur[p]["sol"][:, :DN_DK] - w_s[p] for p in pairs}
        for p in pairs:
            c = cur[p]
            c["o_ref"][c["rs"], c["cs"]] = q_s[p] + _dot(c["qk"] * c["dec"], v_new[p])
        for p in pairs:
            c = cur[p]
            last = DN_CHUNK - 1 if p[0] == 0 else 0
            gl = c["gcc"][last:last + 1, :]
            kh = c["k_ref"][c["rs"], c["cs"]]
            state[p] = state[p] * jnp.exp(gl) + _dot_tn(kh * jnp.exp(gl - c["gcc"]), v_new[p])
    for p in pairs:
        s_scr[p[0], p[1]] = state[p]

    @pl.when((meta_ref[4, j] == 1) & is_ctx)
    def _():
        sfin_ref[...] = s_scr[...]


def _deltanet(qkv, gates, state0, a_log, dt_bias, geom):
    n_tok = qkv.shape[0]
    width = DN_HEADS * DN_DK
    rows = []
    for b in range(geom.b_ctx + geom.b_lat):
        is_ctx = b < geom.b_ctx
        t_seq = geom.t_ctx if is_ctx else geom.t_lat
        base = (b * geom.t_ctx if is_ctx else geom.n_ctx + (b - geom.b_ctx) * geom.t_lat) // DN_WIN
        n_win = t_seq // DN_WIN
        for w in range(n_win):
            rows.append((base + w, base + n_win - 1 - w, b, int(w == 0), int(w == n_win - 1)))
    meta = jnp.asarray(np.array(rows, np.int32).T)
    n_steps = len(rows)

    consts = [_dn_constants(False), _dn_constants(True)]
    masks = jnp.asarray(np.stack([c[0] for c in consts]))
    cum_col = jnp.asarray(np.stack([c[1] for c in consts]))
    cum_row = jnp.asarray(np.stack([c[2] for c in consts]))
    lanes = jnp.arange(LANE)
    pa, pdt = [], []
    for d in range(2):
        hd = jnp.clip(lanes - (16 + 8 * d), 0, DN_HEADS - 1)
        valid = (lanes >= 16 + 8 * d) & (lanes < 24 + 8 * d)
        pa.append(jnp.where(valid, -jnp.exp(a_log[d].astype(F32))[hd], 0.0))
        pdt.append(jnp.where(valid, dt_bias[d].astype(F32)[hd], 0.0))
    pa = jnp.stack(pa).reshape(2, 1, LANE)
    pdt = jnp.stack(pdt).reshape(2, 1, LANE)

    one = pl.Buffered(1)
    n_ctx_seq = geom.b_ctx
    st_shape = (2, DN_HEADS, DN_DK, DN_DK)

    def win(row, col):
        return pl.BlockSpec((DN_WIN, width if col < 3 else LANE), lambda j, meta: (meta[row, j], col % 3))

    def const(a):
        nd = a.ndim
        return pl.BlockSpec(a.shape, lambda j, meta: (0,) * nd, pipeline_mode=one)

    grid_spec = pltpu.PrefetchScalarGridSpec(
        num_scalar_prefetch=1,
        grid=(n_steps,),
        in_specs=[win(0, 0), win(0, 1), win(0, 2), win(0, 3), win(1, 0), win(1, 1), win(1, 2), win(1, 3),
                  pl.BlockSpec((None,) + st_shape, lambda j, meta: (jnp.maximum(meta[2, j] - n_ctx_seq, 0), 0, 0, 0, 0)),
                  const(pa), const(pdt), const(masks), const(cum_col), const(cum_row)],
        out_specs=[pl.BlockSpec((DN_WIN, width), lambda j, meta: (meta[0, j], 0)),
                   pl.BlockSpec((DN_WIN, width), lambda j, meta: (meta[1, j], 0)),
                   pl.BlockSpec((None,) + st_shape, lambda j, meta: (jnp.minimum(meta[2, j], n_ctx_seq - 1), 0, 0, 0, 0))],
        scratch_shapes=[pltpu.VMEM(st_shape, F32)],
    )
    return pl.pallas_call(
        functools.partial(_dn_kernel, n_ctx_seq=n_ctx_seq),
        grid_spec=grid_spec,
        out_shape=[jax.ShapeDtypeStruct((n_tok, width), F32), jax.ShapeDtypeStruct((n_tok, width), F32),
                   jax.ShapeDtypeStruct((n_ctx_seq,) + st_shape, F32)],
        compiler_params=_cparams(("arbitrary",)),
        name="deltanet",
    )(meta, qkv, qkv, qkv, gates, qkv, qkv, qkv, gates, state0, pa, pdt, masks, cum_col, cum_row)


def _gmlp_kernel(u_ref, vg_ref, nw_ref, ws_ref, bias_ref, o_ref, *, tb):
    vg = _gelu(vg_ref[...])
    ms = jnp.mean(vg * vg, axis=-1, keepdims=True)
    vn = (vg * lax.rsqrt(ms + EPS) * nw_ref[...]).astype(BF16)
    for ch in range(tb // GM_CHUNK):
        rs = slice(ch * GM_CHUNK, (ch + 1) * GM_CHUNK)
        for g in range(GM_GROUPS):
            cs = slice(g * GM_CH, (g + 1) * GM_CH)
            mixed = jnp.dot(ws_ref[g], vn[rs, cs], preferred_element_type=F32) + bias_ref[g]
            o_ref[rs, cs] = (_gelu(u_ref[rs, cs]) * mixed).astype(BF16)


def _gmlp(proj, gm_norm, w_s, b_s, u_blk, vg_blk):
    n_tok = proj.shape[0]
    width = GM_GROUPS * GM_CH
    tb = 256
    bias = jnp.broadcast_to(b_s.astype(F32)[:, :, None], (GM_GROUPS, GM_CHUNK, GM_CH))
    return pl.pallas_call(
        functools.partial(_gmlp_kernel, tb=tb),
        grid=(n_tok // tb,),
        in_specs=[pl.BlockSpec((tb, width), lambda i: (i, u_blk)),
                  pl.BlockSpec((tb, width), lambda i: (i, vg_blk)),
                  pl.BlockSpec((1, width), lambda i: (0, 0)),
                  pl.BlockSpec((GM_GROUPS, GM_CHUNK, GM_CHUNK), lambda i: (0, 0, 0)),
                  pl.BlockSpec((GM_GROUPS, GM_CHUNK, GM_CH), lambda i: (0, 0, 0))],
        out_specs=pl.BlockSpec((tb, width), lambda i: (i, 0)),
        out_shape=jax.ShapeDtypeStruct((n_tok, width), BF16),
        compiler_params=_cparams(("parallel",)),
        name="gmlp_gate",
    )(proj, proj, gm_norm.reshape(1, width), w_s.astype(BF16), bias)


def _merge_kernel(of_ref, ob_ref, z_ref, sb_ref, ga_ref, gb_ref, dnw_ref, wa_ref, wb_ref, o_ref, on_scr):
    for h in range(DN_HEADS):
        cs = slice(h * DN_DK, (h + 1) * DN_DK)
        o = of_ref[:, cs] + ob_ref[:, cs]
        ms = jnp.mean(o * o, axis=-1, keepdims=True)
        z = z_ref[:, cs]
        on_scr[:, cs] = (o * lax.rsqrt(ms + EPS) * dnw_ref[...] * (z * _sigmoid(z))).astype(BF16)
    p_a = jnp.dot(on_scr[...], wa_ref[...], preferred_element_type=F32)
    p_b = jnp.dot(sb_ref[...], wb_ref[...], preferred_element_type=F32)
    o_ref[...] = (_sigmoid(ga_ref[...]) * p_a + _sigmoid(gb_ref[...]) * p_b).astype(BF16)


def _merge(o_f, o_b, proj, s_b, dn_norm, w_a, w_b, z_blk, ga_blk, gb_blk):
    n_tok = proj.shape[0]
    wa_in, D = w_a.shape
    wb_in = w_b.shape[0]
    tb = 256
    return pl.pallas_call(
        _merge_kernel,
        grid=(n_tok // tb,),
        in_specs=[pl.BlockSpec((tb, wa_in), lambda i: (i, 0)),
                  pl.BlockSpec((tb, wa_in), lambda i: (i, 0)),
                  pl.BlockSpec((tb, wa_in), lambda i: (i, z_blk)),
                  pl.BlockSpec((tb, wb_in), lambda i: (i, 0)),
                  pl.BlockSpec((tb, D), lambda i: (i, ga_blk)),
                  pl.BlockSpec((tb, D), lambda i: (i, gb_blk)),
                  pl.BlockSpec((1, DN_DK), lambda i: (0, 0)),
                  pl.BlockSpec((wa_in, D), lambda i: (0, 0)),
                  pl.BlockSpec((wb_in, D), lambda i: (0, 0))],
        out_specs=pl.BlockSpec((tb, D), lambda i: (i, 0)),
        out_shape=jax.ShapeDtypeStruct((n_tok, D), BF16),
        scratch_shapes=[pltpu.VMEM((tb, wa_in), BF16)],
        compiler_params=_cparams(("parallel",)),
        name="branch_merge",
    )(o_f, o_b, proj, s_b, proj, proj, dn_norm.reshape(1, DN_DK), w_a.astype(BF16), w_b.astype(BF16))


def _outproj_kernel(m_ref, w_ref, x_ref, gt_ref, o_ref):
    y = jnp.dot(m_ref[...], w_ref[...], preferred_element_type=F32)
    o_ref[...] = x_ref[...] + gt_ref[...] * y


def _out_projection(merged, w_out, x, mod, geom):
    n_tok, D = x.shape
    tb = geom.block(512)
    return pl.pallas_call(
        _outproj_kernel,
        grid=(n_tok // tb,),
        in_specs=[pl.BlockSpec((tb, D), lambda i: (i, 0)),
                  pl.BlockSpec((D, D), lambda i: (0, 0)),
                  pl.BlockSpec((tb, D), lambda i: (i, 0)),
                  _mod_block(geom, tb, 2, D, 1)],
        out_specs=pl.BlockSpec((tb, D), lambda i: (i, 0)),
        out_shape=jax.ShapeDtypeStruct((n_tok, D), F32),
        compiler_params=_cparams(("parallel",)),
        name="out_proj_residual",
    )(merged, w_out.astype(BF16), x, mod)


def _top_rows(cur, k):
    vals = []
    for _ in range(k):
        m = jnp.max(cur, axis=0, keepdims=True)
        vals.append(m)
        cur = jnp.where(cur == m, -jnp.inf, cur)
    return vals


_PAIRS = [(a, b) for a in range(PEER_TOPK) for b in range(PEER_TOPK) if (a + 1) * (b + 1) <= PEER_TOPK + 1]


def _peer_select_kernel(x_ref, nw_ref, sc_ref, sh_ref, wq_ref, keys_ref,
                        ht_ref, thr_ref, s1_ref, a_ref, b_ref):
    x = x_ref[...]
    ms = jnp.mean(x * x, axis=-1, keepdims=True)
    y = x * lax.rsqrt(ms + EPS) * nw_ref[...]
    h2_t = (y * (1.0 + sc_ref[...]) + sh_ref[...]).T.astype(BF16)
    ht_ref[...] = h2_t
    q_t = jnp.dot(wq_ref[...], h2_t, preferred_element_type=F32)
    dk = PEER_NKEYS
    for h in range(PEER_HEADS):
        svs = []
        scores = []
        for p in range(2):
            r0 = (h * 2 + p) * dk
            s = jnp.dot(keys_ref[h * 2 + p], q_t[r0:r0 + dk, :].astype(BF16), preferred_element_type=F32)
            svs.append(_top_rows(s, PEER_TOPK))
            scores.append(s)
        cand = jnp.concatenate([svs[0][a] + svs[1][b] for a, b in _PAIRS], axis=0)
        best = _top_rows(cand, PEER_TOPK + 1)
        m0 = svs[0][0]
        m1 = svs[1][0]
        mx = best[0]
        z = jnp.zeros_like(mx)
        for v in best[:PEER_TOPK]:
            z = z + jnp.exp(v - mx)
        theta = 0.5 * (best[PEER_TOPK - 1] + best[PEER_TOPK])
        thr_ref[h] = theta - scores[0]
        s1_ref[h] = scores[1]
        a_ref[h] = jnp.exp(scores[0] - m0)
        b_ref[h] = jnp.exp(scores[1] - m1) * (0.5 * jnp.exp(m0 + m1 - mx) / z)


def _peer_select(x, norm_w, mod, wq_t, keys, geom):
    n_tok, D = x.shape
    tb = geom.block(256)
    H, NK = PEER_HEADS, PEER_NKEYS
    big = jax.ShapeDtypeStruct((H, NK, n_tok), F32)
    big_spec = pl.BlockSpec((H, NK, tb), lambda i: (0, 0, i))
    return pl.pallas_call(
        _peer_select_kernel,
        grid=(n_tok // tb,),
        in_specs=[pl.BlockSpec((tb, D), lambda i: (i, 0)),
                  pl.BlockSpec((1, D), lambda i: (0, 0)),
                  _mod_block(geom, tb, 4, D, 1),
                  _mod_block(geom, tb, 3, D, 1),
                  pl.BlockSpec(wq_t.shape, lambda i: (0, 0)),
                  pl.BlockSpec(keys.shape, lambda i: (0, 0, 0))],
        out_specs=[pl.BlockSpec((D, tb), lambda i: (0, i)), big_spec, big_spec, big_spec, big_spec],
        out_shape=[jax.ShapeDtypeStruct((D, n_tok), BF16), big, big, big, big],
        compiler_params=_cparams(("parallel",)),
        name="peer_select",
    )(x, norm_w.reshape(1, D), mod, mod, wq_t, keys)


PEER_EBLK = 1024
PEER_TBLK = 512
_PD_ROWS = 64
_PD_IGRP = 4
_GELU_C1 = math.sqrt(2.0 / math.pi)
_GELU_C2 = _GELU_C1 * 0.044715


def _peer_dense_kernel(ht_ref, u_ref, vt_ref, thr_ref, a_ref, s1_ref, b_ref, x_ref, gt_ref, *rest, n_e, tb, n_ctx_blk):
    if n_ctx_blk is None:
        o_ref, acc_scr, act0, act1, w0, w1 = rest
    else:
        fw_ref, oc_ref, ol_ref, acc_scr, act0, act1, w0, w1 = rest
    s = pl.program_id(0)
    e_c = jnp.maximum(s - 2, 0) % n_e

    @pl.when(s == 0)
    def _():
        for r in (act0, act1, w0, w1):
            r[...] = jnp.zeros_like(r)

    @pl.when(e_c == 0)
    def _():
        acc_scr[...] = jnp.zeros_like(acc_scr)

    def stages(act_w, act_r, w_w, w_r):
        act_w[...] = jnp.dot(u_ref[...], ht_ref[...], preferred_element_type=F32)
        acc_scr[...] += jnp.dot(vt_ref[...], w_r[...], preferred_element_type=F32)
        nk = PEER_NKEYS
        for tc in range(tb // LANE):
            cs = slice(tc * LANE, (tc + 1) * LANE)
            for jh in range(nk // _PD_ROWS):
                js = slice(jh * _PD_ROWS, (jh + 1) * _PD_ROWS)
                for ig in range(PEER_EBLK // nk // _PD_IGRP):
                    g = [None] * _PD_IGRP
                    for h in range(PEER_HEADS):
                        s1t = s1_ref[h, js, cs]
                        bt = b_ref[h, js, cs]
                        for k in range(_PD_IGRP):
                            ii = ig * _PD_IGRP + k
                            sel = s1t >= thr_ref[h, ii:ii + 1, cs]
                            gh = jnp.where(sel, bt * a_ref[h, ii:ii + 1, cs], 0.0)
                            g[k] = gh if g[k] is None else g[k] + gh
                    for k in range(_PD_IGRP):
                        r0 = (ig * _PD_IGRP + k) * nk + jh * _PD_ROWS
                        xa = act_r[r0:r0 + _PD_ROWS, cs]
                        tt = jnp.tanh(xa * (_GELU_C1 + _GELU_C2 * (xa * xa)))
                        w_w[r0:r0 + _PD_ROWS, cs] = ((xa + xa * tt) * g[k]).astype(BF16)

    @pl.when(s % 2 == 0)
    def _():
        stages(act0, act1, w1, w0)

    @pl.when(s % 2 == 1)
    def _():
        stages(act1, act0, w0, w1)

    done = (e_c == n_e - 1) & (s >= 2)
    if n_ctx_blk is None:
        @pl.when(done)
        def _():
            o_ref[...] = x_ref[...] + gt_ref[...] * acc_scr[...].T
    else:
        t_c = jnp.maximum(s - 2, 0) // n_e

        def normed():
            xn = x_ref[...] + gt_ref[...] * acc_scr[...].T
            ms = jnp.mean(xn * xn, axis=-1, keepdims=True)
            return xn * lax.rsqrt(ms + EPS) * fw_ref[...]

        @pl.when(done & (t_c < n_ctx_blk))
        def _():
            oc_ref[...] = normed()

        @pl.when(done & (t_c >= n_ctx_blk))
        def _():
            ol_ref[...] = normed()


def _peer_dense(h2_t, u_bf16, vt_bf16, thr, s1, a, b_half, x, mod, geom, final_w=None):
    n_tok, D = x.shape
    n_exp = u_bf16.shape[0]
    tb = geom.block(PEER_TBLK)
    n_e = n_exp // PEER_EBLK
    n_steps = (n_tok // tb) * n_e
    H, NK = PEER_HEADS, PEER_NKEYS
    ib = PEER_EBLK // NK
    row = geom.mod_row(tb)
    one = pl.Buffered(1)
    n_ctx_blk = geom.n_ctx // tb

    def step(s, lag):
        return jnp.clip(s - lag, 0, n_steps - 1)

    def tok(s, lag):
        return step(s, lag) // n_e

    def exp(s, lag):
        return step(s, lag) % n_e

    in_specs = [pl.BlockSpec((D, tb), lambda s: (0, tok(s, 0))),
                pl.BlockSpec((PEER_EBLK, D), lambda s: (exp(s, 0), 0)),
                pl.BlockSpec((D, PEER_EBLK), lambda s: (0, exp(s, 2))),
                pl.BlockSpec((H, ib, tb), lambda s: (0, exp(s, 1), tok(s, 1))),
                pl.BlockSpec((H, ib, tb), lambda s: (0, exp(s, 1), tok(s, 1))),
                pl.BlockSpec((H, NK, tb), lambda s: (0, 0, tok(s, 1)), pipeline_mode=one),
                pl.BlockSpec((H, NK, tb), lambda s: (0, 0, tok(s, 1)), pipeline_mode=one),
                pl.BlockSpec((tb, D), lambda s: (tok(s, 2), 0), pipeline_mode=one),
                pl.BlockSpec((None, None, 1, D), lambda s: (row(tok(s, 2)), 5, 0, 0))]
    args = [h2_t, u_bf16, vt_bf16, thr, a, s1, b_half, x, mod]
    if final_w is None:
        out_specs = pl.BlockSpec((tb, D), lambda s: (tok(s, 2), 0))
        out_shape = jax.ShapeDtypeStruct((n_tok, D), F32)
    else:
        in_specs.append(pl.BlockSpec((1, D), lambda s: (0, 0)))
        args.append(final_w.reshape(1, D))
        out_specs = [pl.BlockSpec((tb, D), lambda s: (jnp.minimum(tok(s, 2), n_ctx_blk - 1), 0), pipeline_mode=one),
                     pl.BlockSpec((tb, D), lambda s: (jnp.maximum(tok(s, 2) - n_ctx_blk, 0), 0), pipeline_mode=one)]
        out_shape = [jax.ShapeDtypeStruct((geom.n_ctx, D), F32), jax.ShapeDtypeStruct((geom.n_lat, D), F32)]
    return pl.pallas_call(
        functools.partial(_peer_dense_kernel, n_e=n_e, tb=tb, n_ctx_blk=None if final_w is None else n_ctx_blk),
        grid=(n_steps + 2,),
        in_specs=in_specs,
        out_specs=out_specs,
        out_shape=out_shape,
        scratch_shapes=[pltpu.VMEM((D, tb), F32),
                        pltpu.VMEM((PEER_EBLK, tb), F32), pltpu.VMEM((PEER_EBLK, tb), F32),
                        pltpu.VMEM((PEER_EBLK, tb), BF16), pltpu.VMEM((PEER_EBLK, tb), BF16)],
        compiler_params=_cparams(("arbitrary",)),
        name="peer_dense",
    )(*args)


def _grid_pos_embed(T, D):
    rows = T // GRID_W
    r, col = jnp.meshgrid(jnp.arange(rows, dtype=F32), jnp.arange(GRID_W, dtype=F32), indexing='ij')
    quarter = D // 4
    freq = 1.0 / (10000.0 ** (jnp.arange(quarter, dtype=F32) / quarter))

    def enc(p):
        ang = p.reshape(-1, 1) * freq
        return jnp.concatenate([jnp.sin(ang), jnp.cos(ang)], axis=-1)

    return jnp.concatenate([enc(r), enc(col)], axis=-1)


def kernel(x_prompt, x_sample, state_delta, c, c_ctx, w_mod, b_mod, norm_mix, w_in, conv_w, dn_a_log,
           dn_dt_bias, dn_norm, gm_norm, gm_w_s, gm_b_s, w_branch_a, w_branch_b, w_out, norm_ffn,
           peer_w_query, peer_sub_keys, peer_u, peer_v, norm_final):
    b_ctx, t_ctx, D = x_prompt.shape
    b_lat, t_lat, _ = x_sample.shape
    depth = w_mod.shape[0]
    geom = _Geom(b_ctx, t_ctx, b_lat, t_lat)
    qk_w = DN_HEADS * DN_DK
    gm_w = GM_GROUPS * GM_CH
    assert t_ctx % CONV_BLK == 0 and t_lat % CONV_BLK == 0 and qk_w == gm_w == PROJ_BLK and D == 2 * qk_w

    xs = x_sample + _grid_pos_embed(t_lat, D).astype(x_sample.dtype)[None]
    x = jnp.concatenate([x_prompt.reshape(geom.n_ctx, D), xs.reshape(geom.n_lat, D)], axis=0)

    n_rows = -(-(1 + b_lat) // SUBLANE) * SUBLANE
    cond = jnp.concatenate([c_ctx[None, :], c, jnp.zeros((n_rows - 1 - b_lat, D), c.dtype)], axis=0)
    mod_all = _modulation(cond, w_mod, b_mod).reshape(depth, n_rows, N_MOD, 1, D)

    z_blk, u_blk, vg_blk = 3, 4, 5
    ga_blk, gb_blk = 3, 4

    new_states = []
    for l in range(depth):
        mod = mod_all[l]
        proj, gates = _in_projection(x, norm_mix[l], mod, w_in[l], geom)
        qkv = _conv_qkv(proj, conv_w[l], geom)
        o_f, o_b, s_ctx = _deltanet(qkv, gates, state_delta[:, l], dn_a_log[l], dn_dt_bias[l], geom)
        new_states.append(s_ctx)
        s_b = _gmlp(proj, gm_norm[l], gm_w_s[l], gm_b_s[l], u_blk, vg_blk)
        merged = _merge(o_f, o_b, proj, s_b, dn_norm[l], w_branch_a[l], w_branch_b[l], z_blk, ga_blk, gb_blk)
        x = _out_projection(merged, w_out[l], x, mod, geom)
        wq_t = peer_w_query[l].reshape(D, -1).T.astype(BF16)
        keys = peer_sub_keys[l].reshape(PEER_HEADS * 2, PEER_NKEYS, -1).astype(BF16)
        h2_t, thr, s1, a, b = _peer_select(x, norm_ffn[l], mod, wq_t, keys, geom)
        x = _peer_dense(h2_t, peer_u[l].astype(BF16), peer_v[l].T.astype(BF16), thr, s1, a, b, x, mod, geom,
                        final_w=norm_final if l == depth - 1 else None)

    y_prompt = x[0].reshape(b_ctx, t_ctx, D)
    y_sample = x[1].reshape(b_lat, t_lat, D)
    new_state_delta = jnp.stack(new_states, axis=1).astype(x_prompt.dtype)
    return (y_prompt, y_sample, new_state_delta)
```

```python
import functools
import math

import numpy as np
import jax
import jax.numpy as jnp
from jax import lax
from jax.experimental import pallas as pl
from jax.experimental.pallas import tpu as pltpu

F32 = jnp.float32
BF16 = jnp.bfloat16

EPS = 1e-6
N_MOD = 6
GRID_W = 64
DN_HEADS = 8
DN_DK = 128
DN_CHUNK = 128
GM_GROUPS = 8
GM_CH = 128
GM_CHUNK = 128
PEER_HEADS = 8
PEER_NKEYS = 128
PEER_TOPK = 16
LANE = 128
SUBLANE = 8
VMEM_LIMIT = 56 * 1024 * 1024
NEG_BIG = -1e30

_HI = lax.Precision.HIGHEST


def _cparams(sem):
    return pltpu.CompilerParams(dimension_semantics=sem, vmem_limit_bytes=VMEM_LIMIT)


def _pick(n, cap):
    b = 1
    while b * 2 <= cap and n % (b * 2) == 0:
        b *= 2
    return b


def _dot(a, b):
    return jnp.dot(a.astype(BF16), b.astype(BF16), preferred_element_type=F32)


def _dot_nt(a, b):
    return lax.dot_general(a.astype(BF16), b.astype(BF16), (((1,), (1,)), ((), ())),
                           preferred_element_type=F32)


def _dot_tn(a, b):
    return lax.dot_general(a.astype(BF16), b.astype(BF16), (((0,), (0,)), ((), ())),
                           preferred_element_type=F32)


def _gelu(x):
    return 0.5 * x * (1.0 + jnp.tanh(math.sqrt(2.0 / math.pi) * (x + 0.044715 * (x * x * x))))


def _sigmoid(x):
    return 1.0 / (1.0 + jnp.exp(-x))


def _softplus(x):
    return jnp.maximum(x, 0.0) + jnp.log(1.0 + jnp.exp(-jnp.abs(x)))


def _mod_kernel(c_ref, w_ref, b_ref, o_ref):
    c = c_ref[...]
    o_ref[...] = _dot(c * _sigmoid(c), w_ref[...]) + b_ref[...]


def _modulation(cond, w_mod, b_mod):
    L, D, N = w_mod.shape
    R = cond.shape[0]
    tn = _pick(N, 1024)
    return pl.pallas_call(
        _mod_kernel,
        grid=(L, N // tn),
        in_specs=[pl.BlockSpec((R, D), lambda l, j: (0, 0)),
                  pl.BlockSpec((None, D, tn), lambda l, j: (l, 0, j)),
                  pl.BlockSpec((None, 1, tn), lambda l, j: (l, 0, j))],
        out_specs=pl.BlockSpec((None, R, tn), lambda l, j: (l, 0, j)),
        out_shape=jax.ShapeDtypeStruct((L, R, N), F32),
        compiler_params=_cparams(("parallel", "parallel")),
        name="adaln_mod",
    )(cond, w_mod, b_mod.reshape(L, 1, N))


class _Geom:
    def __init__(self, b_ctx, t_ctx, b_lat, t_lat):
        self.b_ctx, self.t_ctx, self.b_lat, self.t_lat = b_ctx, t_ctx, b_lat, t_lat
        self.n_ctx = b_ctx * t_ctx
        self.n_lat = b_lat * t_lat
        self.n_tok = self.n_ctx + self.n_lat

    def block(self, cap):
        return _pick(math.gcd(self.n_ctx, self.t_lat), cap)

    def mod_row(self, tb):
        n_ctx_blocks = self.n_ctx // tb
        per_seq = self.t_lat // tb

        def row(i):
            return jnp.where(i < n_ctx_blocks, 0, 1 + (i - n_ctx_blocks) // per_seq)

        return row


def _mod_block(geom, tb, which, D, ngrid):
    row = geom.mod_row(tb)
    if ngrid == 1:
        return pl.BlockSpec((None, None, 1, D), lambda i: (row(i), which, 0, 0))
    return pl.BlockSpec((None, None, 1, D), lambda i, j: (row(i), which, 0, 0))


PROJ_BLK = 1024
N_GATES = 4 * DN_HEADS


def _inproj_kernel(x_ref, nw_ref, sc_ref, sh_ref, wa_ref, wb_ref, ws_ref, o_ref, sm_ref, h_scr, *, n_a):
    j = pl.program_id(1)

    @pl.when(j == 0)
    def _():
        x = x_ref[...]
        ms = jnp.mean(x * x, axis=-1, keepdims=True)
        y = x * lax.rsqrt(ms + EPS) * nw_ref[...]
        h_scr[...] = (y * (1.0 + sc_ref[...]) + sh_ref[...]).astype(BF16)
        sm_ref[...] = jnp.dot(h_scr[...], ws_ref[...], preferred_element_type=F32)

    @pl.when(j < n_a)
    def _():
        o_ref[...] = jnp.dot(h_scr[...], wa_ref[...], preferred_element_type=F32)

    @pl.when(j >= n_a)
    def _():
        o_ref[...] = jnp.dot(h_scr[...], wb_ref[...], preferred_element_type=F32)


def _in_projection(x, norm_w, mod, w_in_l, geom):
    n_tok, D = x.shape
    w = w_in_l.astype(BF16)
    o_small = 4 * PROJ_BLK
    n_a = o_small // PROJ_BLK
    w_b = w[:, o_small + N_GATES:]
    n_b = w_b.shape[1] // PROJ_BLK
    w_s = jnp.pad(w[:, o_small:o_small + N_GATES], ((0, 0), (0, LANE - N_GATES)))
    tb = geom.block(1024)
    return pl.pallas_call(
        functools.partial(_inproj_kernel, n_a=n_a),
        grid=(n_tok // tb, n_a + n_b),
        in_specs=[pl.BlockSpec((tb, D), lambda i, j: (i, 0)),
                  pl.BlockSpec((1, D), lambda i, j: (0, 0)),
                  _mod_block(geom, tb, 1, D, 2),
                  _mod_block(geom, tb, 0, D, 2),
                  pl.BlockSpec((D, PROJ_BLK), lambda i, j: (0, jnp.minimum(j, n_a - 1))),
                  pl.BlockSpec((D, PROJ_BLK), lambda i, j: (0, jnp.maximum(j - n_a, 0))),
                  pl.BlockSpec((D, LANE), lambda i, j: (0, 0))],
        out_specs=[pl.BlockSpec((tb, PROJ_BLK), lambda i, j: (i, j)),
                   pl.BlockSpec((tb, LANE), lambda i, j: (i, 0))],
        out_shape=[jax.ShapeDtypeStruct((n_tok, (n_a + n_b) * PROJ_BLK), F32),
                   jax.ShapeDtypeStruct((n_tok, LANE), F32)],
        scratch_shapes=[pltpu.VMEM((tb, D), BF16)],
        compiler_params=_cparams(("parallel", "arbitrary")),
        name="norm_in_proj",
    )(x, norm_w.reshape(1, D), mod, mod, w, w_b, w_s)


CONV_BLK = 256


CONV_ROWS = 64


def _conv_kernel(cur_ref, prev_ref, next_ref, w_ref, o_ref, *, geom):
    i = pl.program_id(0)
    j = pl.program_id(1)
    n_ctx_blocks = geom.n_ctx // CONV_BLK
    bps_ctx = geom.t_ctx // CONV_BLK
    bps_lat = geom.t_lat // CONV_BLK
    pos = jnp.where(i < n_ctx_blocks, i % bps_ctx, (i - n_ctx_blocks) % bps_lat)
    per = jnp.where(i < n_ctx_blocks, bps_ctx, bps_lat)
    before_blk = jnp.where(pos == 0, 0.0, prev_ref[SUBLANE - 1:SUBLANE, :])
    after_blk = jnp.where(pos == per - 1, 0.0, next_ref[0:1, :])
    w = w_ref[...]
    n_chunks = CONV_BLK // CONV_ROWS
    rid = lax.broadcasted_iota(jnp.int32, (CONV_ROWS, w.shape[1]), 0)

    def conv_silu(c):
        r0 = c * CONV_ROWS
        x = cur_ref[r0:r0 + CONV_ROWS, :]
        row_before = before_blk if c == 0 else cur_ref[r0 - 1:r0, :]
        row_after = after_blk if c == n_chunks - 1 else cur_ref[r0 + CONV_ROWS:r0 + CONV_ROWS + 1, :]
        xm1 = jnp.where(rid == 0, row_before, pltpu.roll(x, 1, axis=0))
        xp1 = jnp.where(rid == CONV_ROWS - 1, row_after, pltpu.roll(x, CONV_ROWS - 1, axis=0))
        y = w[0:1, :] * xm1 + w[1:2, :] * x + w[2:3, :] * xp1
        return y * _sigmoid(y)

    @pl.when(j == 2)
    def _():
        for c in range(n_chunks):
            o_ref[c * CONV_ROWS:(c + 1) * CONV_ROWS, :] = conv_silu(c).astype(BF16)

    @pl.when(j < 2)
    def _():
        scale = jnp.where(j == 0, DN_DK ** -0.5, 1.0).astype(F32)
        for c in range(n_chunks):
            y = conv_silu(c)
            for h in range(DN_HEADS):
                seg = y[:, h * DN_DK:(h + 1) * DN_DK]
                ss = jnp.sum(seg * seg, axis=-1, keepdims=True)
                o_ref[c * CONV_ROWS:(c + 1) * CONV_ROWS, h * DN_DK:(h + 1) * DN_DK] = (
                    seg * (lax.rsqrt(ss + EPS) * scale)).astype(BF16)


def _conv_qkv(proj, conv_w, geom):
    n_tok = proj.shape[0]
    width = DN_HEADS * DN_DK
    nb8 = n_tok // SUBLANE
    r = CONV_BLK // SUBLANE
    return pl.pallas_call(
        functools.partial(_conv_kernel, geom=geom),
        grid=(n_tok // CONV_BLK, 3),
        in_specs=[pl.BlockSpec((CONV_BLK, width), lambda i, j: (i, j)),
                  pl.BlockSpec((SUBLANE, width), lambda i, j: (jnp.maximum(i * r - 1, 0), j)),
                  pl.BlockSpec((SUBLANE, width), lambda i, j: (jnp.minimum((i + 1) * r, nb8 - 1), j)),
                  pl.BlockSpec((3, width), lambda i, j: (0, j))],
        out_specs=pl.BlockSpec((CONV_BLK, width), lambda i, j: (i, j)),
        out_shape=jax.ShapeDtypeStruct((n_tok, 3 * width), BF16),
        compiler_params=_cparams(("parallel", "parallel")),
        name="conv_silu_l2",
    )(proj, proj, proj, conv_w)


DN_WIN = 256
_LEVELS = tuple(1 << i for i in range(DN_CHUNK.bit_length() - 1))


def _dn_constants(reverse):
    n = DN_CHUNK
    r = np.arange(n)[:, None]
    c = np.arange(n)[None, :]
    before = (c > r) if reverse else (c < r)
    incl = before | (r == c)
    lvl = []
    for s in _LEVELS:
        same = (r // (2 * s)) == (c // (2 * s))
        if reverse:
            m = same & ((r % (2 * s)) < s) & ((c % (2 * s)) >= s)
        else:
            m = same & ((r % (2 * s)) >= s) & ((c % (2 * s)) < s)
        lvl.append(m)
    masks = np.stack([before, incl] + lvl).astype(np.float32)
    cum_col = incl.astype(np.float32)
    return masks, cum_col, cum_col.T


def _dn_kernel(meta_ref, qf_ref, kf_ref, vf_ref, smf_ref, qb_ref, kb_ref, vb_ref, smb_ref, s0_ref,
               pa_ref, pdt_ref, msk_ref, cc_ref, cr_ref, of_ref, ob_ref, sfin_ref, s_scr, *, n_ctx_seq):
    j = pl.program_id(0)
    seq = meta_ref[2, j]
    is_ctx = seq < n_ctx_seq

    @pl.when(meta_ref[3, j] == 1)
    def _():
        s_scr[...] = jnp.where(is_ctx, 0.0, s0_ref[...])

    n_chunks = DN_WIN // DN_CHUNK
    dirs = ((0, qf_ref, kf_ref, vf_ref, smf_ref, of_ref), (1, qb_ref, kb_ref, vb_ref, smb_ref, ob_ref))
    ch = {}
    for d, q_ref, k_ref, v_ref, sm_ref, o_ref in dirs:
        sm = sm_ref[...]
        g_all = pa_ref[d] * _softplus(sm + pdt_ref[d])
        beta_all = _sigmoid(sm)
        g_t = g_all.T
        before = msk_ref[d, 0]
        incl = msk_ref[d, 1]
        for c in range(n_chunks):
            rs = slice(c * DN_CHUNK, (c + 1) * DN_CHUNK)
            gc_col = jnp.dot(cc_ref[d], g_all[rs, :], precision=_HI, preferred_element_type=F32)
            gc_row = jnp.dot(g_t[:, rs], cr_ref[d], precision=_HI, preferred_element_type=F32)
            for h in range(DN_HEADS):
                la = 16 + 8 * d + h
                lb = 8 * d + h
                cs = slice(h * DN_DK, (h + 1) * DN_DK)
                gcc = gc_col[:, la:la + 1]
                gcr = gc_row[la:la + 1, :]
                bcol = beta_all[rs, lb:lb + 1]
                dec = jnp.exp(jnp.where(incl > 0, gcc - gcr, NEG_BIG))
                kh = k_ref[rs, cs]
                m = (bcol * _dot_nt(kh, kh)) * (dec * before)
                ch[d, c, h] = dict(d=d, rs=rs, cs=cs, gcc=gcc, bcol=bcol, dec=dec, m=m, o_ref=o_ref,
                                   q_ref=q_ref, k_ref=k_ref, v_ref=v_ref,
                                   t=(incl - before) - m * msk_ref[d, 2])
    chains = list(ch.values())
    for li in range(1, len(_LEVELS)):
        tmp = [_dot(c["t"], c["m"] * msk_ref[c["d"], 2 + li]) for c in chains]
        for c, x in zip(chains, tmp):
            c["t"] = c["t"] - _dot(x, c["t"])
    for c in chains:
        kh = c["k_ref"][c["rs"], c["cs"]]
        egc = jnp.exp(c["gcc"])
        rhs = jnp.concatenate([c["v_ref"][c["rs"], c["cs"]] * c["bcol"], kh * (c["bcol"] * egc)], axis=1)
        c["sol"] = _dot(c["t"], rhs)
        c["egc"] = egc
        c["qk"] = _dot_nt(c["q_ref"][c["rs"], c["cs"]], kh)
    pairs = [(d, h) for d in range(2) for h in range(DN_HEADS)]
    state = {p: s_scr[p[0], p[1]] for p in pairs}
    for step in range(n_chunks):
        cur = {(d, h): ch[d, step if d == 0 else n_chunks - 1 - step, h] for d, h in pairs}
        w_s = {p: _dot(cur[p]["sol"][:, DN_DK:], state[p]) for p in pairs}
        q_s = {p: _dot(cur[p]["q_ref"][cur[p]["rs"], cur[p]["cs"]] * cur[p]["egc"], state[p]) for p in pairs}
        v_new = {p: cur[p]["sol"][:, :DN_DK] - w_s[p] for p in pairs}
        for p in pairs:
            c = cur[p]
            c["o_ref"][c["rs"], c["cs"]] = q_s[p] + _dot(c["qk"] * c["dec"], v_new[p])
        for p in pairs:
            c = cur[p]
            last = DN_CHUNK - 1 if p[0] == 0 else 0
            gl = c["gcc"][last:last + 1, :]
            kh = c["k_ref"][c["rs"], c["cs"]]
            state[p] = state[p] * jnp.exp(gl) + _dot_tn(kh * jnp.exp(gl - c["gcc"]), v_new[p])
    for p in pairs:
        s_scr[p[0], p[1]] = state[p]

    @pl.when((meta_ref[4, j] == 1) & is_ctx)
    def _():
        sfin_ref[...] = s_scr[...]


def _deltanet(qkv, gates, state0, a_log, dt_bias, geom):
    n_tok = qkv.shape[0]
    width = DN_HEADS * DN_DK
    rows = []
    for b in range(geom.b_ctx + geom.b_lat):
        is_ctx = b < geom.b_ctx
        t_seq = geom.t_ctx if is_ctx else geom.t_lat
        base = (b * geom.t_ctx if is_ctx else geom.n_ctx + (b - geom.b_ctx) * geom.t_lat) // DN_WIN
        n_win = t_seq // DN_WIN
        for w in range(n_win):
            rows.append((base + w, base + n_win - 1 - w, b, int(w == 0), int(w == n_win - 1)))
    meta = jnp.asarray(np.array(rows, np.int32).T)
    n_steps = len(rows)

    consts = [_dn_constants(False), _dn_constants(True)]
    masks = jnp.asarray(np.stack([c[0] for c in consts]))
    cum_col = jnp.asarray(np.stack([c[1] for c in consts]))
    cum_row = jnp.asarray(np.stack([c[2] for c in consts]))
    lanes = jnp.arange(LANE)
    pa, pdt = [], []
    for d in range(2):
        hd = jnp.clip(lanes - (16 + 8 * d), 0, DN_HEADS - 1)
        valid = (lanes >= 16 + 8 * d) & (lanes < 24 + 8 * d)
        pa.append(jnp.where(valid, -jnp.exp(a_log[d].astype(F32))[hd], 0.0))
        pdt.append(jnp.where(valid, dt_bias[d].astype(F32)[hd], 0.0))
    pa = jnp.stack(pa).reshape(2, 1, LANE)
    pdt = jnp.stack(pdt).reshape(2, 1, LANE)

    one = pl.Buffered(1)
    n_ctx_seq = geom.b_ctx
    st_shape = (2, DN_HEADS, DN_DK, DN_DK)

    def win(row, col):
        return pl.BlockSpec((DN_WIN, width if col < 3 else LANE), lambda j, meta: (meta[row, j], col % 3))

    def const(a):
        nd = a.ndim
        return pl.BlockSpec(a.shape, lambda j, meta: (0,) * nd, pipeline_mode=one)

    grid_spec = pltpu.PrefetchScalarGridSpec(
        num_scalar_prefetch=1,
        grid=(n_steps,),
        in_specs=[win(0, 0), win(0, 1), win(0, 2), win(0, 3), win(1, 0), win(1, 1), win(1, 2), win(1, 3),
                  pl.BlockSpec((None,) + st_shape, lambda j, meta: (jnp.maximum(meta[2, j] - n_ctx_seq, 0), 0, 0, 0, 0)),
                  const(pa), const(pdt), const(masks), const(cum_col), const(cum_row)],
        out_specs=[pl.BlockSpec((DN_WIN, width), lambda j, meta: (meta[0, j], 0)),
                   pl.BlockSpec((DN_WIN, width), lambda j, meta: (meta[1, j], 0)),
                   pl.BlockSpec((None,) + st_shape, lambda j, meta: (jnp.minimum(meta[2, j], n_ctx_seq - 1), 0, 0, 0, 0))],
        scratch_shapes=[pltpu.VMEM(st_shape, F32)],
    )
    return pl.pallas_call(
        functools.partial(_dn_kernel, n_ctx_seq=n_ctx_seq),
        grid_spec=grid_spec,
        out_shape=[jax.ShapeDtypeStruct((n_tok, width), F32), jax.ShapeDtypeStruct((n_tok, width), F32),
                   jax.ShapeDtypeStruct((n_ctx_seq,) + st_shape, F32)],
        compiler_params=_cparams(("arbitrary",)),
        name="deltanet",
    )(meta, qkv, qkv, qkv, gates, qkv, qkv, qkv, gates, state0, pa, pdt, masks, cum_col, cum_row)


def _gmlp_kernel(u_ref, vg_ref, nw_ref, ws_ref, bias_ref, o_ref, *, tb):
    vg = _gelu(vg_ref[...])
    ms = jnp.mean(vg * vg, axis=-1, keepdims=True)
    vn = (vg * lax.rsqrt(ms + EPS) * nw_ref[...]).astype(BF16)
    for ch in range(tb // GM_CHUNK):
        rs = slice(ch * GM_CHUNK, (ch + 1) * GM_CHUNK)
        for g in range(GM_GROUPS):
            cs = slice(g * GM_CH, (g + 1) * GM_CH)
            mixed = jnp.dot(ws_ref[g], vn[rs, cs], preferred_element_type=F32) + bias_ref[g]
            o_ref[rs, cs] = (_gelu(u_ref[rs, cs]) * mixed).astype(BF16)


def _gmlp(proj, gm_norm, w_s, b_s, u_blk, vg_blk):
    n_tok = proj.shape[0]
    width = GM_GROUPS * GM_CH
    tb = 256
    bias = jnp.broadcast_to(b_s.astype(F32)[:, :, None], (GM_GROUPS, GM_CHUNK, GM_CH))
    return pl.pallas_call(
        functools.partial(_gmlp_kernel, tb=tb),
        grid=(n_tok // tb,),
        in_specs=[pl.BlockSpec((tb, width), lambda i: (i, u_blk)),
                  pl.BlockSpec((tb, width), lambda i: (i, vg_blk)),
                  pl.BlockSpec((1, width), lambda i: (0, 0)),
                  pl.BlockSpec((GM_GROUPS, GM_CHUNK, GM_CHUNK), lambda i: (0, 0, 0)),
                  pl.BlockSpec((GM_GROUPS, GM_CHUNK, GM_CH), lambda i: (0, 0, 0))],
        out_specs=pl.BlockSpec((tb, width), lambda i: (i, 0)),
        out_shape=jax.ShapeDtypeStruct((n_tok, width), BF16),
        compiler_params=_cparams(("parallel",)),
        name="gmlp_gate",
    )(proj, proj, gm_norm.reshape(1, width), w_s.astype(BF16), bias)


def _merge_kernel(of_ref, ob_ref, z_ref, sb_ref, ga_ref, gb_ref, dnw_ref, wa_ref, wb_ref, o_ref, on_scr):
    for h in range(DN_HEADS):
        cs = slice(h * DN_DK, (h + 1) * DN_DK)
        o = of_ref[:, cs] + ob_ref[:, cs]
        ms = jnp.mean(o * o, axis=-1, keepdims=True)
        z = z_ref[:, cs]
        on_scr[:, cs] = (o * lax.rsqrt(ms + EPS) * dnw_ref[...] * (z * _sigmoid(z))).astype(BF16)
    p_a = jnp.dot(on_scr[...], wa_ref[...], preferred_element_type=F32)
    p_b = jnp.dot(sb_ref[...], wb_ref[...], preferred_element_type=F32)
    o_ref[...] = (_sigmoid(ga_ref[...]) * p_a + _sigmoid(gb_ref[...]) * p_b).astype(BF16)


def _merge(o_f, o_b, proj, s_b, dn_norm, w_a, w_b, z_blk, ga_blk, gb_blk):
    n_tok = proj.shape[0]
    wa_in, D = w_a.shape
    wb_in = w_b.shape[0]
    tb = 256
    return pl.pallas_call(
        _merge_kernel,
        grid=(n_tok // tb,),
        in_specs=[pl.BlockSpec((tb, wa_in), lambda i: (i, 0)),
                  pl.BlockSpec((tb, wa_in), lambda i: (i, 0)),
                  pl.BlockSpec((tb, wa_in), lambda i: (i, z_blk)),
                  pl.BlockSpec((tb, wb_in), lambda i: (i, 0)),
                  pl.BlockSpec((tb, D), lambda i: (i, ga_blk)),
                  pl.BlockSpec((tb, D), lambda i: (i, gb_blk)),
                  pl.BlockSpec((1, DN_DK), lambda i: (0, 0)),
                  pl.BlockSpec((wa_in, D), lambda i: (0, 0)),
                  pl.BlockSpec((wb_in, D), lambda i: (0, 0))],
        out_specs=pl.BlockSpec((tb, D), lambda i: (i, 0)),
        out_shape=jax.ShapeDtypeStruct((n_tok, D), BF16),
        scratch_shapes=[pltpu.VMEM((tb, wa_in), BF16)],
        compiler_params=_cparams(("parallel",)),
        name="branch_merge",
    )(o_f, o_b, proj, s_b, proj, proj, dn_norm.reshape(1, DN_DK), w_a.astype(BF16), w_b.astype(BF16))


def _outproj_kernel(m_ref, w_ref, x_ref, gt_ref, o_ref):
    y = jnp.dot(m_ref[...], w_ref[...], preferred_element_type=F32)
    o_ref[...] = x_ref[...] + gt_ref[...] * y


def _out_projection(merged, w_out, x, mod, geom):
    n_tok, D = x.shape
    tb = geom.block(512)
    return pl.pallas_call(
        _outproj_kernel,
        grid=(n_tok // tb,),
        in_specs=[pl.BlockSpec((tb, D), lambda i: (i, 0)),
                  pl.BlockSpec((D, D), lambda i: (0, 0)),
                  pl.BlockSpec((tb, D), lambda i: (i, 0)),
                  _mod_block(geom, tb, 2, D, 1)],
        out_specs=pl.BlockSpec((tb, D), lambda i: (i, 0)),
        out_shape=jax.ShapeDtypeStruct((n_tok, D), F32),
        compiler_params=_cparams(("parallel",)),
        name="out_proj_residual",
    )(merged, w_out.astype(BF16), x, mod)


def _top_rows(cur, k):
    vals = []
    for _ in range(k):
        m = jnp.max(cur, axis=0, keepdims=True)
        vals.append(m)
        cur = jnp.where(cur == m, -jnp.inf, cur)
    return vals


_PAIRS = [(a, b) for a in range(PEER_TOPK) for b in range(PEER_TOPK) if (a + 1) * (b + 1) <= PEER_TOPK + 1]


def _peer_select_kernel(x_ref, nw_ref, sc_ref, sh_ref, wq_ref, keys_ref,
                        ht_ref, thr_ref, s1_ref, a_ref, b_ref):
    x = x_ref[...]
    ms = jnp.mean(x * x, axis=-1, keepdims=True)
    y = x * lax.rsqrt(ms + EPS) * nw_ref[...]
    h2_t = (y * (1.0 + sc_ref[...]) + sh_ref[...]).T.astype(BF16)
    ht_ref[...] = h2_t
    q_t = jnp.dot(wq_ref[...], h2_t, preferred_element_type=F32)
    dk = PEER_NKEYS
    for h in range(PEER_HEADS):
        svs = []
        scores = []
        for p in range(2):
            r0 = (h * 2 + p) * dk
            s = jnp.dot(keys_ref[h * 2 + p], q_t[r0:r0 + dk, :].astype(BF16), preferred_element_type=F32)
            svs.append(_top_rows(s, PEER_TOPK))
            scores.append(s)
        cand = jnp.concatenate([svs[0][a] + svs[1][b] for a, b in _PAIRS], axis=0)
        best = _top_rows(cand, PEER_TOPK + 1)
        m0 = svs[0][0]
        m1 = svs[1][0]
        mx = best[0]
        z = jnp.zeros_like(mx)
        for v in best[:PEER_TOPK]:
            z = z + jnp.exp(v - mx)
        theta = 0.5 * (best[PEER_TOPK - 1] + best[PEER_TOPK])
        thr_ref[h] = theta - scores[0]
        s1_ref[h] = scores[1]
        a_ref[h] = jnp.exp(scores[0] - m0)
        b_ref[h] = jnp.exp(scores[1] - m1) * (0.5 * jnp.exp(m0 + m1 - mx) / z)


def _peer_select(x, norm_w, mod, wq_t, keys, geom):
    n_tok, D = x.shape
    tb = geom.block(256)
    H, NK = PEER_HEADS, PEER_NKEYS
    big = jax.ShapeDtypeStruct((H, NK, n_tok), F32)
    big_spec = pl.BlockSpec((H, NK, tb), lambda i: (0, 0, i))
    return pl.pallas_call(
        _peer_select_kernel,
        grid=(n_tok // tb,),
        in_specs=[pl.BlockSpec((tb, D), lambda i: (i, 0)),
                  pl.BlockSpec((1, D), lambda i: (0, 0)),
                  _mod_block(geom, tb, 4, D, 1),
                  _mod_block(geom, tb, 3, D, 1),
                  pl.BlockSpec(wq_t.shape, lambda i: (0, 0)),
                  pl.BlockSpec(keys.shape, lambda i: (0, 0, 0))],
        out_specs=[pl.BlockSpec((D, tb), lambda i: (0, i)), big_spec, big_spec, big_spec, big_spec],
        out_shape=[jax.ShapeDtypeStruct((D, n_tok), BF16), big, big, big, big],
        compiler_params=_cparams(("parallel",)),
        name="peer_select",
    )(x, norm_w.reshape(1, D), mod, mod, wq_t, keys)


PEER_EBLK = 1024
PEER_TBLK = 512
_PD_ROWS = 16
_PD_IGRP = 4
_GELU_C1 = math.sqrt(2.0 / math.pi)
_GELU_C2 = _GELU_C1 * 0.044715


def _peer_dense_kernel(ht_ref, u_ref, vt_ref, thr_ref, a_ref, s1_ref, b_ref, x_ref, gt_ref, *rest, n_e, tb, n_ctx_blk):
    if n_ctx_blk is None:
        o_ref, acc_scr, act0, act1, w0, w1 = rest
    else:
        fw_ref, oc_ref, ol_ref, acc_scr, act0, act1, w0, w1 = rest
    s = pl.program_id(0)
    e_c = jnp.maximum(s - 2, 0) % n_e

    @pl.when(s == 0)
    def _():
        for r in (act0, act1, w0, w1):
            r[...] = jnp.zeros_like(r)

    @pl.when(e_c == 0)
    def _():
        acc_scr[...] = jnp.zeros_like(acc_scr)

    def stages(act_w, act_r, w_w, w_r):
        act_w[...] = jnp.dot(u_ref[...], ht_ref[...], preferred_element_type=F32)
        acc_scr[...] += jnp.dot(vt_ref[...], w_r[...], preferred_element_type=F32)
        nk = PEER_NKEYS
        for tc in range(tb // LANE):
            cs = slice(tc * LANE, (tc + 1) * LANE)
            for jh in range(nk // _PD_ROWS):
                js = slice(jh * _PD_ROWS, (jh + 1) * _PD_ROWS)
                for ig in range(PEER_EBLK // nk // _PD_IGRP):
                    g = [None] * _PD_IGRP
                    for h in range(PEER_HEADS):
                        s1t = s1_ref[h, js, cs]
                        bt = b_ref[h, js, cs]
                        for k in range(_PD_IGRP):
                            ii = ig * _PD_IGRP + k
                            sel = s1t >= thr_ref[h, ii:ii + 1, cs]
                            gh = jnp.where(sel, bt * a_ref[h, ii:ii + 1, cs], 0.0)
                            g[k] = gh if g[k] is None else g[k] + gh
                    for k in range(_PD_IGRP):
                        r0 = (ig * _PD_IGRP + k) * nk + jh * _PD_ROWS
                        xa = act_r[r0:r0 + _PD_ROWS, cs]
                        tt = jnp.tanh(xa * (_GELU_C1 + _GELU_C2 * (xa * xa)))
                        w_w[r0:r0 + _PD_ROWS, cs] = ((xa + xa * tt) * g[k]).astype(BF16)

    @pl.when(s % 2 == 0)
    def _():
        stages(act0, act1, w1, w0)

    @pl.when(s % 2 == 1)
    def _():
        stages(act1, act0, w0, w1)

    done = (e_c == n_e - 1) & (s >= 2)
    if n_ctx_blk is None:
        @pl.when(done)
        def _():
            o_ref[...] = x_ref[...] + gt_ref[...] * acc_scr[...].T
    else:
        t_c = jnp.maximum(s - 2, 0) // n_e

        def normed():
            xn = x_ref[...] + gt_ref[...] * acc_scr[...].T
            ms = jnp.mean(xn * xn, axis=-1, keepdims=True)
            return xn * lax.rsqrt(ms + EPS) * fw_ref[...]

        @pl.when(done & (t_c < n_ctx_blk))
        def _():
            oc_ref[...] = normed()

        @pl.when(done & (t_c >= n_ctx_blk))
        def _():
            ol_ref[...] = normed()


def _peer_dense(h2_t, u_bf16, vt_bf16, thr, s1, a, b_half, x, mod, geom, final_w=None):
    n_tok, D = x.shape
    n_exp = u_bf16.shape[0]
    tb = geom.block(PEER_TBLK)
    n_e = n_exp // PEER_EBLK
    n_steps = (n_tok // tb) * n_e
    H, NK = PEER_HEADS, PEER_NKEYS
    ib = PEER_EBLK // NK
    row = geom.mod_row(tb)
    one = pl.Buffered(1)
    n_ctx_blk = geom.n_ctx // tb

    def step(s, lag):
        return jnp.clip(s - lag, 0, n_steps - 1)

    def tok(s, lag):
        return step(s, lag) // n_e

    def exp(s, lag):
        return step(s, lag) % n_e

    in_specs = [pl.BlockSpec((D, tb), lambda s: (0, tok(s, 0))),
                pl.BlockSpec((PEER_EBLK, D), lambda s: (exp(s, 0), 0)),
                pl.BlockSpec((D, PEER_EBLK), lambda s: (0, exp(s, 2))),
                pl.BlockSpec((H, ib, tb), lambda s: (0, exp(s, 1), tok(s, 1))),
                pl.BlockSpec((H, ib, tb), lambda s: (0, exp(s, 1), tok(s, 1))),
                pl.BlockSpec((H, NK, tb), lambda s: (0, 0, tok(s, 1))),
                pl.BlockSpec((H, NK, tb), lambda s: (0, 0, tok(s, 1))),
                pl.BlockSpec((tb, D), lambda s: (tok(s, 2), 0), pipeline_mode=one),
                pl.BlockSpec((None, None, 1, D), lambda s: (row(tok(s, 2)), 5, 0, 0))]
    args = [h2_t, u_bf16, vt_bf16, thr, a, s1, b_half, x, mod]
    if final_w is None:
        out_specs = pl.BlockSpec((tb, D), lambda s: (tok(s, 2), 0))
        out_shape = jax.ShapeDtypeStruct((n_tok, D), F32)
    else:
        in_specs.append(pl.BlockSpec((1, D), lambda s: (0, 0)))
        args.append(final_w.reshape(1, D))
        out_specs = [pl.BlockSpec((tb, D), lambda s: (jnp.minimum(tok(s, 2), n_ctx_blk - 1), 0), pipeline_mode=one),
                     pl.BlockSpec((tb, D), lambda s: (jnp.maximum(tok(s, 2) - n_ctx_blk, 0), 0), pipeline_mode=one)]
        out_shape = [jax.ShapeDtypeStruct((geom.n_ctx, D), F32), jax.ShapeDtypeStruct((geom.n_lat, D), F32)]
    return pl.pallas_call(
        functools.partial(_peer_dense_kernel, n_e=n_e, tb=tb, n_ctx_blk=None if final_w is None else n_ctx_blk),
        grid=(n_steps + 2,),
        in_specs=in_specs,
        out_specs=out_specs,
        out_shape=out_shape,
        scratch_shapes=[pltpu.VMEM((D, tb), F32),
                        pltpu.VMEM((PEER_EBLK, tb), F32), pltpu.VMEM((PEER_EBLK, tb), F32),
                        pltpu.VMEM((PEER_EBLK, tb), BF16), pltpu.VMEM((PEER_EBLK, tb), BF16)],
        compiler_params=_cparams(("arbitrary",)),
        name="peer_dense",
    )(*args)


def _grid_pos_embed(T, D):
    rows = T // GRID_W
    r, col = jnp.meshgrid(jnp.arange(rows, dtype=F32), jnp.arange(GRID_W, dtype=F32), indexing='ij')
    quarter = D // 4
    freq = 1.0 / (10000.0 ** (jnp.arange(quarter, dtype=F32) / quarter))

    def enc(p):
        ang = p.reshape(-1, 1) * freq
        return jnp.concatenate([jnp.sin(ang), jnp.cos(ang)], axis=-1)

    return jnp.concatenate([enc(r), enc(col)], axis=-1)


def kernel(x_prompt, x_sample, state_delta, c, c_ctx, w_mod, b_mod, norm_mix, w_in, conv_w, dn_a_log,
           dn_dt_bias, dn_norm, gm_norm, gm_w_s, gm_b_s, w_branch_a, w_branch_b, w_out, norm_ffn,
           peer_w_query, peer_sub_keys, peer_u, peer_v, norm_final):
    b_ctx, t_ctx, D = x_prompt.shape
    b_lat, t_lat, _ = x_sample.shape
    depth = w_mod.shape[0]
    geom = _Geom(b_ctx, t_ctx, b_lat, t_lat)
    qk_w = DN_HEADS * DN_DK
    gm_w = GM_GROUPS * GM_CH
    assert t_ctx % CONV_BLK == 0 and t_lat % CONV_BLK == 0 and qk_w == gm_w == PROJ_BLK and D == 2 * qk_w

    xs = x_sample + _grid_pos_embed(t_lat, D).astype(x_sample.dtype)[None]
    x = jnp.concatenate([x_prompt.reshape(geom.n_ctx, D), xs.reshape(geom.n_lat, D)], axis=0)

    n_rows = -(-(1 + b_lat) // SUBLANE) * SUBLANE
    cond = jnp.concatenate([c_ctx[None, :], c, jnp.zeros((n_rows - 1 - b_lat, D), c.dtype)], axis=0)
    mod_all = _modulation(cond, w_mod, b_mod).reshape(depth, n_rows, N_MOD, 1, D)

    z_blk, u_blk, vg_blk = 3, 4, 5
    ga_blk, gb_blk = 3, 4

    new_states = []
    for l in range(depth):
        mod = mod_all[l]
        proj, gates = _in_projection(x, norm_mix[l], mod, w_in[l], geom)
        qkv = _conv_qkv(proj, conv_w[l], geom)
        o_f, o_b, s_ctx = _deltanet(qkv, gates, state_delta[:, l], dn_a_log[l], dn_dt_bias[l], geom)
        new_states.append(s_ctx)
        s_b = _gmlp(proj, gm_norm[l], gm_w_s[l], gm_b_s[l], u_blk, vg_blk)
        merged = _merge(o_f, o_b, proj, s_b, dn_norm[l], w_branch_a[l], w_branch_b[l], z_blk, ga_blk, gb_blk)
        x = _out_projection(merged, w_out[l], x, mod, geom)
        wq_t = peer_w_query[l].reshape(D, -1).T.astype(BF16)
        keys = peer_sub_keys[l].reshape(PEER_HEADS * 2, PEER_NKEYS, -1).astype(BF16)
        h2_t, thr, s1, a, b = _peer_select(x, norm_ffn[l], mod, wq_t, keys, geom)
        x = _peer_dense(h2_t, peer_u[l].astype(BF16), peer_v[l].T.astype(BF16), thr, s1, a, b, x, mod, geom,
                        final_w=norm_final if l == depth - 1 else None)

    y_prompt = x[0].reshape(b_ctx, t_ctx, D)
    y_sample = x[1].reshape(b_lat, t_lat, D)
    new_state_delta = jnp.stack(new_states, axis=1).astype(x_prompt.dtype)
    return (y_prompt, y_sample, new_state_delta)
```
